```python
import jax, jax.numpy as jnp
from jax import lax
import numpy as np

D_MODEL = 1024
BATCH = 4
SEQ = 4096
DEPTH = 4
DEC_BATCH = 128
DEC_SEQ = 1
PAST_LEN = 8192
PAGE_SIZE = 128

N_MIXERS = 3
LAYER_TYPES = tuple(i % N_MIXERS for i in range(DEPTH))
N_MLA = LAYER_TYPES.count(0)
N_DIL = LAYER_TYPES.count(1)
N_NSA = LAYER_TYPES.count(2)
D_FF = 4 * D_MODEL
RMS_EPS = 1e-6
Q_BLOCK = 128
NEG_INF = -1e30
POOL_NUM = 5
POOL_DEN = 4

MLA_HEADS = 16
MLA_NOPE = 64
MLA_ROPE = 32
MLA_V = 64
MLA_Q_LORA = D_MODEL // 2
MLA_KV_LORA = D_MODEL // 4
ROPE_THETA = 10000.0

DIL_GROUPS = ((128, 1), (512, 4), (2048, 16))
DIL_HEADS = 4
DIL_HD = 128

NSA_HEADS = 16
NSA_KV_HEADS = 2
NSA_HD = 64
NSA_GROUP = NSA_HEADS // NSA_KV_HEADS
CMP_LEN = 32
CMP_STRIDE = 16
SEL_LEN = 64
SEL_TOP = 16
NSA_WIN = 512
FORCE_BONUS = 1e4
NSA_QW = NSA_HEADS * NSA_HD
NSA_KVW = 3 * 2 * NSA_KV_HEADS * NSA_HD
NSA_IN = NSA_QW + NSA_KVW + 3 * NSA_HEADS

kernel_name = "hybrid_mla_dilated_nsa_step"


def rms_norm(x, g):
    xf = x.astype(jnp.float32)
    y = xf * lax.rsqrt(jnp.mean(xf * xf, axis=-1, keepdims=True) + RMS_EPS)
    return (y * g.astype(jnp.float32)).astype(x.dtype)


def alibi_slopes(n):
    return 2.0 ** (-8.0 * jnp.arange(1, n + 1, dtype=jnp.float32) / n)


def masked_softmax(s, mask):
    s = jnp.where(mask, s, NEG_INF)
    m = jnp.max(s, axis=-1, keepdims=True)
    e = jnp.where(mask, jnp.exp(s - m), 0.0)
    den = jnp.maximum(jnp.sum(e, axis=-1, keepdims=True), 1e-30)
    return e / den, m + jnp.log(den)


def apply_rope(x, pos):
    half = x.shape[-1] // 2
    inv = ROPE_THETA ** (-jnp.arange(half, dtype=jnp.float32) / half)
    ang = pos.astype(jnp.float32)[:, None] * inv[None, :]
    cos = jnp.cos(ang)[None, :, None, :]
    sin = jnp.sin(ang)[None, :, None, :]
    xf = x.astype(jnp.float32)
    x1, x2 = xf[..., :half], xf[..., half:]
    return jnp.concatenate([x1 * cos - x2 * sin, x1 * sin + x2 * cos], axis=-1).astype(x.dtype)


def gather_pages(pool, page_table):
    g = pool[page_table]
    return g.reshape(g.shape[0], g.shape[1] * g.shape[2], *g.shape[3:])


def pad_rows(rows, mult):
    pad = (-rows.shape[1]) % mult
    return jnp.pad(rows, [(0, 0), (0, pad)] + [(0, 0)] * (rows.ndim - 2))


def block_sweep(fn, q_inputs, n_q):
    nb = n_q // Q_BLOCK
    split = lambda a: a.reshape(a.shape[0], nb, Q_BLOCK, *a.shape[2:]).swapaxes(0, 1)
    starts = jnp.arange(nb, dtype=jnp.int32) * Q_BLOCK
    out = lax.map(lambda a: fn(*a[0], a[1]), (tuple(split(a) for a in q_inputs), starts))
    out = out.swapaxes(0, 1)
    return out.reshape(out.shape[0], n_q, *out.shape[3:])


def sq_relu_mlp(x, w1, w2):
    return jnp.square(jax.nn.relu(x @ w1)) @ w2


def mla_project(h, pos, w_in, q_norm, w_qb, kv_norm, w_kvb):
    b, t, _ = h.shape
    z = h @ w_in
    c_q = rms_norm(z[..., :MLA_Q_LORA], q_norm)
    c_kv = rms_norm(z[..., MLA_Q_LORA:MLA_Q_LORA + MLA_KV_LORA], kv_norm)
    k_rope = apply_rope(z[..., MLA_Q_LORA + MLA_KV_LORA:][:, :, None, :], pos)[:, :, 0]
    q = (c_q @ w_qb).reshape(b, t, MLA_HEADS, MLA_NOPE + MLA_ROPE)
    q_rope = apply_rope(q[..., MLA_NOPE:], pos)
    q_lat = jnp.einsum('bthn,rhn->bthr', q[..., :MLA_NOPE], w_kvb[..., :MLA_NOPE])
    return q_lat, q_rope, c_kv, k_rope


def mla_attend(q_lat, q_rope, q_pos, c_kv, k_rope, k_pos, w_uv):
    scale = (MLA_NOPE + MLA_ROPE) ** -0.5
    s = (jnp.einsum('bthr,bsr->bhts', q_lat, c_kv)
         + jnp.einsum('bthd,bsd->bhts', q_rope, k_rope)).astype(jnp.float32) * scale
    p, _ = masked_softmax(s, k_pos[None, :] <= q_pos[:, None])
    o_lat = jnp.einsum('bhts,bsr->bthr', p.astype(c_kv.dtype), c_kv)
    return jnp.einsum('bthr,rhv->bthv', o_lat, w_uv)


def mla_mixer(hp, hs, cache_lat, cache_rope, page_table, w_in, q_norm, w_qb, kv_norm, w_kvb, w_o):
    w_uv = w_kvb[..., MLA_NOPE:]
    n_p = hp.shape[1]
    pos_p = jnp.arange(n_p, dtype=jnp.int32)
    ql_p, qr_p, ckv_p, kr_p = mla_project(hp, pos_p, w_in, q_norm, w_qb, kv_norm, w_kvb)
    o_p = block_sweep(
        lambda ql, qr, st: mla_attend(ql, qr, st + jnp.arange(Q_BLOCK, dtype=jnp.int32),
                                      ckv_p, kr_p, pos_p, w_uv),
        (ql_p, qr_p), n_p)
    past = page_table.shape[1] * cache_lat.shape[1]
    n_s = hs.shape[1]
    pos_s = past + jnp.arange(n_s, dtype=jnp.int32)
    ql_s, qr_s, ckv_s, kr_s = mla_project(hs, pos_s, w_in, q_norm, w_qb, kv_norm, w_kvb)
    ckv_all = jnp.concatenate([gather_pages(cache_lat, page_table), ckv_s], axis=1)
    kr_all = jnp.concatenate([gather_pages(cache_rope, page_table), kr_s], axis=1)
    o_s = mla_attend(ql_s, qr_s, pos_s, ckv_all, kr_all,
                     jnp.arange(past + n_s, dtype=jnp.int32), w_uv)
    out = lambda o: o.reshape(o.shape[0], o.shape[1], -1) @ w_o
    return out(o_p), out(o_s), (ckv_p, kr_p, ckv_s, kr_s)


def dil_group_attend(q, q_idx, k, v, window, dil, slopes):
    n_k = window // dil + 1
    off = dil * jnp.arange(n_k, dtype=jnp.int32)
    idx = q_idx[:, None] - off[None, :]
    valid = idx >= 0
    idx = jnp.maximum(idx, 0)
    kg = k[:, idx]
    vg = v[:, idx]
    s = (jnp.einsum('bthd,btkhd->bhtk', q, kg).astype(jnp.float32) * DIL_HD ** -0.5
         - slopes[:, None, None] * off.astype(jnp.float32))
    p, lse = masked_softmax(s, valid[None, None])
    o = jnp.einsum('bhtk,btkhd->bthd', p.astype(vg.dtype), vg)
    return o, lse[..., 0].transpose(0, 2, 1)


def dil_merge(q3, q_idx_list, kv_list, slopes):
    outs, lses = [], []
    for g, (window, dil) in enumerate(DIL_GROUPS):
        o, l = dil_group_attend(q3[:, :, g], q_idx_list[g], kv_list[g][:, :, 0], kv_list[g][:, :, 1],
                                window, dil, slopes[g * DIL_HEADS:(g + 1) * DIL_HEADS])
        outs.append(o)
        lses.append(l)
    alpha = jax.nn.softmax(jnp.stack(lses, axis=-1), axis=-1)
    o = jnp.einsum('bthg,gbthd->bthd', alpha.astype(outs[0].dtype), jnp.stack(outs))
    return o.reshape(o.shape[0], o.shape[1], DIL_HEADS * DIL_HD)


def dil_mixer(hp, hs, caches, w_in, w_o):
    n_g = len(DIL_GROUPS)
    proj = lambda h: (h @ w_in).reshape(h.shape[0], h.shape[1], n_g, 3, DIL_HEADS, DIL_HD)
    slopes = alibi_slopes(n_g * DIL_HEADS)
    n_p = hp.shape[1]
    zp = proj(hp)
    kv_p = [zp[:, :, g, 1:] for g in range(n_g)]
    o_p = block_sweep(
        lambda q, st: dil_merge(q, [st + jnp.arange(Q_BLOCK, dtype=jnp.int32)] * n_g, kv_p, slopes),
        (zp[:, :, :, 0],), n_p)
    st_p = [kv[:, n_p - min(w, n_p):] for kv, (w, _) in zip(kv_p, DIL_GROUPS)]
    n_s = hs.shape[1]
    zs = proj(hs)
    kv_all = [jnp.concatenate([c, zs[:, :, g, 1:]], axis=1) for g, c in enumerate(caches)]
    idx_s = [c.shape[1] + jnp.arange(n_s, dtype=jnp.int32) for c in caches]
    o_s = dil_merge(zs[:, :, :, 0], idx_s, kv_all, slopes)
    st_s = [kv[:, n_s:] for kv in kv_all]
    return o_p @ w_o, o_s @ w_o, st_p, st_s


def nsa_project(h, w_in):
    b, t, _ = h.shape
    z = h @ w_in
    q = z[..., :NSA_QW].reshape(b, t, NSA_HEADS, NSA_HD)
    kv = z[..., NSA_QW:NSA_QW + NSA_KVW].reshape(b, t, 3, 2, NSA_KV_HEADS, NSA_HD)
    gates = jax.nn.sigmoid(z[..., NSA_QW + NSA_KVW:].astype(jnp.float32)).astype(h.dtype)
    return q, kv, gates.reshape(b, t, 3, NSA_HEADS)


def nsa_compress(rows, w, pe):
    b, n = rows.shape[:2]
    r = CMP_LEN // CMP_STRIDE
    rows = pad_rows(rows, CMP_STRIDE)
    n_ch = rows.shape[1] // CMP_STRIDE
    ch = rows.reshape(b, n_ch, CMP_STRIDE, 2, NSA_KV_HEADS, NSA_HD)
    w_r = w.reshape(2, r, CMP_STRIDE, NSA_HD, NSA_HD)
    pe_r = pe.reshape(2, r, CMP_STRIDE, NSA_HD)
    n_cmp = n_ch - r + 1
    comp = 0.0
    for j in range(r):
        part = jnp.einsum('bclkgd,klde->bckge',
                          ch + pe_r[:, j].transpose(1, 0, 2)[:, :, None, :], w_r[:, j])
        comp = comp + part[:, j:j + n_cmp]
    c_end = jnp.arange(n_cmp, dtype=jnp.int32) * CMP_STRIDE + CMP_LEN - 1
    return comp[:, :, 0], comp[:, :, 1], c_end


def nsa_core(q, gates, q_pos, ck, cv, c_end, sk, sv, wk, wv, w_pos, slopes):
    b, t = q.shape[:2]
    G, HG, D = NSA_KV_HEADS, NSA_GROUP, NSA_HD
    scale = D ** -0.5
    qg = q.reshape(b, t, G, HG, D)
    sl = slopes.reshape(G, HG)[:, :, None, None]
    dist_c = (q_pos[:, None] - c_end[None, :]).astype(jnp.float32)
    s_c = jnp.einsum('btghd,bngd->bghtn', qg, ck).astype(jnp.float32) * scale - sl * dist_c
    p_c, _ = masked_softmax(s_c, dist_c >= 0)
    o_c = jnp.einsum('bghtn,bngd->btghd', p_c.astype(cv.dtype), cv)
    n_sel = sk.shape[1] // SEL_LEN
    c_start = c_end - (CMP_LEN - 1)
    s_start = jnp.arange(n_sel, dtype=jnp.int32) * SEL_LEN
    overlap = jnp.maximum(jnp.minimum(c_start[:, None] + CMP_LEN, s_start[None, :] + SEL_LEN)
                          - jnp.maximum(c_start[:, None], s_start[None, :]), 0).astype(jnp.float32) / CMP_STRIDE
    imp = jnp.einsum('bghtn,nj->bgtj', p_c, overlap)
    blk = jnp.arange(n_sel, dtype=jnp.int32)[None, :]
    cur = (q_pos // SEL_LEN)[:, None]
    forced = (blk == 0) | (blk == cur) | (blk == cur - 1)
    score = jnp.where(forced, imp + FORCE_BONUS, imp)
    score = jnp.where(s_start[None, :] <= q_pos[:, None], score, -1.0)
    k_top = min(SEL_TOP, n_sel)
    _, idx = lax.top_k(score, k_top)
    skb = sk.reshape(b, n_sel, SEL_LEN, G, D).transpose(0, 3, 1, 2, 4)
    svb = sv.reshape(b, n_sel, SEL_LEN, G, D).transpose(0, 3, 1, 2, 4)
    bi = jnp.arange(b)[:, None, None, None]
    gi = jnp.arange(G)[None, :, None, None]
    kg = skb[bi, gi, idx].reshape(b, G, t, k_top * SEL_LEN, D)
    vg = svb[bi, gi, idx].reshape(b, G, t, k_top * SEL_LEN, D)
    k_pos = (idx[..., None] * SEL_LEN + jnp.arange(SEL_LEN, dtype=jnp.int32)).reshape(b, G, t, k_top * SEL_LEN)
    dist_s = (q_pos[None, None, :, None] - k_pos).astype(jnp.float32)[:, :, None]
    s_s = jnp.einsum('btghd,bgtkd->bghtk', qg, kg).astype(jnp.float32) * scale - sl * dist_s
    p_s, _ = masked_softmax(s_s, dist_s >= 0)
    o_s = jnp.einsum('bghtk,bgtkd->btghd', p_s.astype(vg.dtype), vg)
    dist_w = q_pos[:, None] - w_pos[None, :]
    mask_w = (dist_w >= 0) & (dist_w <= NSA_WIN) & (w_pos[None, :] >= 0)
    s_w = jnp.einsum('btghd,bngd->bghtn', qg, wk).astype(jnp.float32) * scale - sl * dist_w.astype(jnp.float32)
    p_w, _ = masked_softmax(s_w, mask_w)
    o_w = jnp.einsum('bghtn,bngd->btghd', p_w.astype(wv.dtype), wv)
    gt = gates.reshape(b, t, 3, G, HG)[..., None]
    o = gt[:, :, 0] * o_c + gt[:, :, 1] * o_s + gt[:, :, 2] * o_w
    return o.reshape(b, t, NSA_HEADS * D)


def nsa_mixer(hp, hs, cache_cmp, cache_sel, cache_win, page_table, w_in, cmp_w, cmp_pe, w_o):
    slopes = alibi_slopes(NSA_HEADS)
    n_p = hp.shape[1]
    q_p, kv_p, g_p = nsa_project(hp, w_in)
    ck_p, cv_p, ce_p = nsa_compress(kv_p[:, :, 0], cmp_w, cmp_pe)
    sel_p = pad_rows(kv_p[:, :, 1], SEL_LEN)
    win_p = jnp.pad(kv_p[:, :, 2], ((0, 0), (NSA_WIN, 0), (0, 0), (0, 0), (0, 0)))

    def blk(q, g, st):
        q_pos = st + jnp.arange(Q_BLOCK, dtype=jnp.int32)
        wr = lax.dynamic_slice_in_dim(win_p, st, NSA_WIN + Q_BLOCK, axis=1)
        w_pos = st - NSA_WIN + jnp.arange(NSA_WIN + Q_BLOCK, dtype=jnp.int32)
        return nsa_core(q, g, q_pos, ck_p, cv_p, ce_p, sel_p[:, :, 0], sel_p[:, :, 1],
                        wr[:, :, 0], wr[:, :, 1], w_pos, slopes)

    o_p = block_sweep(blk, (q_p, g_p), n_p)
    st_p = (kv_p[:, :, 0], kv_p[:, :, 1], kv_p[:, n_p - min(NSA_WIN, n_p):, 2])
    n_s = hs.shape[1]
    past = page_table.shape[1] * cache_cmp.shape[1]
    pos_s = past + jnp.arange(n_s, dtype=jnp.int32)
    q_s, kv_s, g_s = nsa_project(hs, w_in)
    cmp_all = jnp.concatenate([gather_pages(cache_cmp, page_table), kv_s[:, :, 0]], axis=1)
    ck_s, cv_s, ce_s = nsa_compress(cmp_all, cmp_w, cmp_pe)
    sel_all = pad_rows(jnp.concatenate([gather_pages(cache_sel, page_table), kv_s[:, :, 1]], axis=1), SEL_LEN)
    win_all = jnp.concatenate([cache_win, kv_s[:, :, 2]], axis=1)
    n_w = cache_win.shape[1]
    w_pos = past - n_w + jnp.arange(n_w + n_s, dtype=jnp.int32)
    o_s = nsa_core(q_s, g_s, pos_s, ck_s, cv_s, ce_s, sel_all[:, :, 0], sel_all[:, :, 1],
                   win_all[:, :, 0], win_all[:, :, 1], w_pos, slopes)
    st_s = (kv_s[:, :, 0], kv_s[:, :, 1], win_all[:, n_s:])
    return o_p @ w_o, o_s @ w_o, st_p, st_s


def setup_inputs(seed: int = 0) -> dict:
    key = jax.random.key(seed)
    ks = jax.random.split(key, 32)
    f32 = jnp.float32
    nrm = lambda k, shape, scale: jax.random.normal(k, shape, f32) * scale
    n_pages = PAST_LEN // PAGE_SIZE
    n_used = DEC_BATCH * n_pages
    n_phys = (POOL_NUM * n_used + POOL_DEN - 1) // POOL_DEN
    page_table = jax.random.permutation(ks[0], n_phys)[:n_used].reshape(DEC_BATCH, n_pages).astype(jnp.int32)
    dl = [min(w, PAST_LEN) for w, _ in DIL_GROUPS]
    wl = min(NSA_WIN, PAST_LEN)
    return {
        "x_prompt": nrm(ks[1], (BATCH, SEQ, D_MODEL), 1.0),
        "x_sample": nrm(ks[2], (DEC_BATCH, DEC_SEQ, D_MODEL), 1.0),
        "cache_mla_latent": nrm(ks[3], (N_MLA, n_phys, PAGE_SIZE, MLA_KV_LORA), 1.0),
        "cache_mla_rope": nrm(ks[4], (N_MLA, n_phys, PAGE_SIZE, MLA_ROPE), 1.0),
        "cache_dil_kv0": nrm(ks[5], (N_DIL, DEC_BATCH, dl[0], 2, DIL_HEADS, DIL_HD), 1.0),
        "cache_dil_kv1": nrm(ks[6], (N_DIL, DEC_BATCH, dl[1], 2, DIL_HEADS, DIL_HD), 1.0),
        "cache_dil_kv2": nrm(ks[7], (N_DIL, DEC_BATCH, dl[2], 2, DIL_HEADS, DIL_HD), 1.0),
        "cache_nsa_cmp": nrm(ks[8], (N_NSA, n_phys, PAGE_SIZE, 2, NSA_KV_HEADS, NSA_HD), 1.0),
        "cache_nsa_sel": nrm(ks[9], (N_NSA, n_phys, PAGE_SIZE, 2, NSA_KV_HEADS, NSA_HD), 1.0),
        "cache_nsa_win": nrm(ks[10], (N_NSA, DEC_BATCH, wl, 2, NSA_KV_HEADS, NSA_HD), 1.0),
        "page_table": page_table,
        "norm_gains": 1.0 + nrm(ks[11], (DEPTH, 4, D_MODEL), 0.02),
        "mlp_w1": nrm(ks[12], (DEPTH, D_MODEL, D_FF), D_MODEL ** -0.5),
        "mlp_w2": nrm(ks[13], (DEPTH, D_FF, D_MODEL), D_FF ** -0.5),
        "mla_w_in": nrm(ks[14], (N_MLA, D_MODEL, MLA_Q_LORA + MLA_KV_LORA + MLA_ROPE), D_MODEL ** -0.5),
        "mla_q_norm": 1.0 + nrm(ks[15], (N_MLA, MLA_Q_LORA), 0.02),
        "mla_w_qb": nrm(ks[16], (N_MLA, MLA_Q_LORA, MLA_HEADS * (MLA_NOPE + MLA_ROPE)), MLA_Q_LORA ** -0.5),
        "mla_kv_norm": 1.0 + nrm(ks[17], (N_MLA, MLA_KV_LORA), 0.02),
        "mla_w_kvb": nrm(ks[18], (N_MLA, MLA_KV_LORA, MLA_HEADS, MLA_NOPE + MLA_V), MLA_KV_LORA ** -0.5),
        "mla_w_o": nrm(ks[19], (N_MLA, MLA_HEADS * MLA_V, D_MODEL), (MLA_HEADS * MLA_V) ** -0.5),
        "dil_w_in": nrm(ks[20], (N_DIL, D_MODEL, len(DIL_GROUPS) * 3 * DIL_HEADS * DIL_HD), D_MODEL ** -0.5),
        "dil_w_o": nrm(ks[21], (N_DIL, DIL_HEADS * DIL_HD, D_MODEL), (DIL_HEADS * DIL_HD) ** -0.5),
        "nsa_w_in": nrm(ks[22], (N_NSA, D_MODEL, NSA_IN), D_MODEL ** -0.5),
        "nsa_cmp_w": nrm(ks[23], (N_NSA, 2, CMP_LEN, NSA_HD, NSA_HD), (CMP_LEN * NSA_HD) ** -0.5),
        "nsa_cmp_pe": nrm(ks[24], (N_NSA, 2, CMP_LEN, NSA_HD), 0.1),
        "nsa_w_o": nrm(ks[25], (N_NSA, NSA_HEADS * NSA_HD, D_MODEL), (NSA_HEADS * NSA_HD) ** -0.5),
    }


def reference(x_prompt, x_sample, cache_mla_latent, cache_mla_rope, cache_dil_kv0, cache_dil_kv1,
              cache_dil_kv2, cache_nsa_cmp, cache_nsa_sel, cache_nsa_win, page_table, norm_gains,
              mlp_w1, mlp_w2, mla_w_in, mla_q_norm, mla_w_qb, mla_kv_norm, mla_w_kvb, mla_w_o,
              dil_w_in, dil_w_o, nsa_w_in, nsa_cmp_w, nsa_cmp_pe, nsa_w_o):
    xp, xs = x_prompt, x_sample
    mla_st = [[] for _ in range(4)]
    dil_st = [[] for _ in range(6)]
    nsa_st = [[] for _ in range(6)]
    ia = ib = ic = 0
    for i in range(DEPTH):
        g = norm_gains[i]
        hp, hs = rms_norm(xp, g[0]), rms_norm(xs, g[0])
        kind = LAYER_TYPES[i]
        if kind == 0:
            mp, ms, st = mla_mixer(hp, hs, cache_mla_latent[ia], cache_mla_rope[ia], page_table,
                                   mla_w_in[ia], mla_q_norm[ia], mla_w_qb[ia], mla_kv_norm[ia],
                                   mla_w_kvb[ia], mla_w_o[ia])
            for lst, a in zip(mla_st, st):
                lst.append(a)
            ia += 1
        elif kind == 1:
            mp, ms, st_p, st_s = dil_mixer(hp, hs, (cache_dil_kv0[ib], cache_dil_kv1[ib], cache_dil_kv2[ib]),
                                           dil_w_in[ib], dil_w_o[ib])
            for lst, a in zip(dil_st, (*st_p, *st_s)):
                lst.append(a)
            ib += 1
        else:
            mp, ms, st_p, st_s = nsa_mixer(hp, hs, cache_nsa_cmp[ic], cache_nsa_sel[ic], cache_nsa_win[ic],
                                           page_table, nsa_w_in[ic], nsa_cmp_w[ic], nsa_cmp_pe[ic], nsa_w_o[ic])
            for lst, a in zip(nsa_st, (*st_p, *st_s)):
                lst.append(a)
            ic += 1
        xp = xp + rms_norm(mp, g[1])
        xs = xs + rms_norm(ms, g[1])
        xp = xp + rms_norm(sq_relu_mlp(rms_norm(xp, g[2]), mlp_w1[i], mlp_w2[i]), g[3])
        xs = xs + rms_norm(sq_relu_mlp(rms_norm(xs, g[2]), mlp_w1[i], mlp_w2[i]), g[3])
    return (xp, xs,
            jnp.stack(mla_st[0]), jnp.stack(mla_st[1]), jnp.stack(mla_st[2]), jnp.stack(mla_st[3]),
            jnp.stack(dil_st[0]), jnp.stack(dil_st[1]), jnp.stack(dil_st[2]),
            jnp.stack(dil_st[3]), jnp.stack(dil_st[4]), jnp.stack(dil_st[5]),
            jnp.stack(nsa_st[0]), jnp.stack(nsa_st[1]), jnp.stack(nsa_st[2]),
            jnp.stack(nsa_st[3]), jnp.stack(nsa_st[4]), jnp.stack(nsa_st[5]))
```

```python
import functools
import math
from typing import NamedTuple, Optional

import numpy as np
import jax
import jax.numpy as jnp
from jax import lax
from jax.experimental import pallas as pl
from jax.experimental.pallas import tpu as pltpu

F32 = jnp.float32
BF16 = jnp.bfloat16
I32 = jnp.int32

D_MODEL = 1024
D_FF = 4 * D_MODEL
RMS_EPS = 1e-6
NEG_INF = -1e30
PAGE = 128
MLA_HEADS, MLA_NOPE, MLA_ROPE, MLA_V = 16, 64, 32, 64
MLA_Q_LORA, MLA_KV_LORA = D_MODEL // 2, D_MODEL // 4
ROPE_THETA = 10000.0
DIL_GROUPS = ((128, 1), (512, 4), (2048, 16))
DIL_HEADS, DIL_HD = 4, 128
NSA_HEADS, NSA_KV_HEADS, NSA_HD = 16, 2, 64
NSA_GROUP = NSA_HEADS // NSA_KV_HEADS
CMP_LEN, CMP_STRIDE, SEL_LEN, SEL_TOP, NSA_WIN = 32, 16, 64, 16, 512
FORCE_BONUS = 1e4
LANES = 128
V7X_VMEM_CAP = 60 << 20


def _cparams(sem, vmem_bytes):
    lim = int(min(max(vmem_bytes * 5 // 4 + (4 << 20), 32 << 20), V7X_VMEM_CAP))
    return pltpu.CompilerParams(dimension_semantics=sem, vmem_limit_bytes=lim)


def _row_tile(n, pref):
    t = min(n, pref)
    while n % t:
        t //= 2
    return t


def _rms(x, g):
    ms = jnp.mean(x * x, axis=-1, keepdims=True)
    return (x * lax.rsqrt(ms + RMS_EPS)) * g


def _rms_proj_kernel(x_ref, g_ref, w_ref, o_ref):
    xn = _rms(x_ref[...], g_ref[...]).astype(BF16)
    o_ref[...] = jnp.dot(xn, w_ref[...], preferred_element_type=F32).astype(o_ref.dtype)


def rms_proj(x, g, w, tn, out_dtype=F32):
    T, K = x.shape
    N = w.shape[1]
    tm = _row_tile(T, 512)
    est = 2 * (tm * K * 4 + K * tn * 2 + tm * tn * 4)
    return pl.pallas_call(
        _rms_proj_kernel,
        grid=(T // tm, N // tn),
        in_specs=[pl.BlockSpec((tm, K), lambda i, j: (i, 0)),
                  pl.BlockSpec((1, K), lambda i, j: (0, 0)),
                  pl.BlockSpec((K, tn), lambda i, j: (0, j))],
        out_specs=pl.BlockSpec((tm, tn), lambda i, j: (i, j)),
        out_shape=jax.ShapeDtypeStruct((T, N), out_dtype),
        compiler_params=_cparams(("parallel", "parallel"), est),
        name="rms_proj",
    )(x, g.reshape(1, K), w)


def _mla_prep_kernel(x_ref, g_ref, wq_ref, wkv_ref, wr_ref, wrs_ref, qn_ref, kvn_ref,
                     wqb_ref, wqbs_ref, wuk_ref, wuv_ref, cq_ref, sq_ref, ck_ref, sk_ref,
                     ckv_ref, kr_ref, q_ref, k_ref, v_ref):
    xn = _rms(x_ref[...], g_ref[...]).astype(BF16)
    dot = functools.partial(jnp.dot, preferred_element_type=F32)
    cq = _rms(dot(xn, wq_ref[...]), qn_ref[...]).astype(BF16)
    ckv = _rms(dot(xn, wkv_ref[...]), kvn_ref[...])
    ckv_ref[...] = ckv
    ckv_b = ckv.astype(BF16)
    kr = dot(xn, wr_ref[...]) * ck_ref[...] + dot(xn, wrs_ref[...]) * sk_ref[...]
    kr_ref[...] = kr
    q = dot(cq, wqb_ref[...])
    qs = dot(cq, wqbs_ref[...])
    kn = dot(ckv_b, wuk_ref[...])
    cqt, sqt = cq_ref[...], sq_ref[...]
    for h in range(MLA_HEADS):
        sl = slice(h * LANES, (h + 1) * LANES)
        q_ref[:, sl] = (q[:, sl] * cqt + qs[:, sl] * sqt).astype(BF16)
        k_ref[:, sl] = (kn[:, sl] + kr).astype(BF16)
    v_ref[...] = dot(ckv_b, wuv_ref[...]).astype(BF16)


def mla_prep(x, g, w, tabs):
    T = x.shape[0]
    tm = _row_tile(T, 512)
    HP = MLA_HEADS * LANES
    row = lambda n: pl.BlockSpec((tm, n), lambda i: (i, 0))
    full = lambda a: pl.BlockSpec(a.shape, lambda i: (0,) * a.ndim)
    weights = [g.reshape(1, -1), w["wq"], w["wkv"], w["wr"], w["wrs"], w["qn"], w["kvn"],
               w["wqb"], w["wqbs"], w["wuk"], w["wuv"]]
    est = 2 * sum(int(a.size) * a.dtype.itemsize for a in weights) + 2 * tm * (4096 + 4 * 512 + 1024 + 512 + 3 * 2 * HP) + 6 * tm * HP * 4
    return pl.pallas_call(
        _mla_prep_kernel,
        grid=(T // tm,),
        in_specs=[row(D_MODEL)] + [full(a) for a in weights] + [row(LANES)] * 4,
        out_specs=[row(MLA_KV_LORA), row(LANES), row(HP), row(HP), row(HP)],
        out_shape=[jax.ShapeDtypeStruct((T, MLA_KV_LORA), F32), jax.ShapeDtypeStruct((T, LANES), F32),
                   jax.ShapeDtypeStruct((T, HP), BF16), jax.ShapeDtypeStruct((T, HP), BF16),
                   jax.ShapeDtypeStruct((T, HP), BF16)],
        compiler_params=_cparams(("parallel",), est),
        name="mla_prep",
    )(x, *weights, *tabs)


def _nsa_proj_kernel(x_ref, g_ref, wq_ref, wkv_ref, wkvp_ref, wg_ref,
                     q_ref, cmp_ref, sel_ref, win_ref, kvp_ref, gate_ref):
    xn = _rms(x_ref[...], g_ref[...]).astype(BF16)
    dot = functools.partial(jnp.dot, preferred_element_type=F32)
    q_ref[...] = dot(xn, wq_ref[...]).astype(BF16)
    kv = dot(xn, wkv_ref[...])
    cmp_ref[...] = kv[:, 0:256]
    sel_ref[...] = kv[:, 256:512]
    win_ref[...] = kv[:, 512:768]
    kvp_ref[...] = dot(xn, wkvp_ref[...]).astype(BF16)
    gate_ref[...] = jax.nn.sigmoid(dot(xn, wg_ref[...]))


def nsa_proj(x, g, w):
    T = x.shape[0]
    tm = _row_tile(T, 512)
    row = lambda n: pl.BlockSpec((tm, n), lambda i: (i, 0))
    full = lambda a: pl.BlockSpec(a.shape, lambda i: (0,) * a.ndim)
    weights = [g.reshape(1, -1), w["wq"], w["wkv"], w["wkvp"], w["wg"]]
    est = 2 * sum(int(a.size) * a.dtype.itemsize for a in weights) + 2 * tm * (4096 + 2 * 2048 + 4 * 768 + 2 * 1024 + 512) + 4 * tm * 4096
    return pl.pallas_call(
        _nsa_proj_kernel,
        grid=(T // tm,),
        in_specs=[row(D_MODEL)] + [full(a) for a in weights],
        out_specs=[row(NSA_HEADS * LANES), row(256), row(256), row(256), row(8 * LANES), row(LANES)],
        out_shape=[jax.ShapeDtypeStruct((T, NSA_HEADS * LANES), BF16)] + [jax.ShapeDtypeStruct((T, 256), F32)] * 3
                  + [jax.ShapeDtypeStruct((T, 8 * LANES), BF16), jax.ShapeDtypeStruct((T, LANES), F32)],
        compiler_params=_cparams(("parallel",), est),
        name="nsa_proj",
    )(x, *weights)


def _oproj_kernel(*refs, mode):
    if mode == "plain":
        o_ref, w_ref, g_ref, x_ref, out_ref = refs
        o = o_ref[...]
    elif mode == "dil":
        o0, o1, o2, l0, l1, l2, w_ref, g_ref, x_ref, out_ref = refs
        a0, a1, a2 = l0[...], l1[...], l2[...]
        m = jnp.maximum(jnp.maximum(a0, a1), a2)
        e0, e1, e2 = jnp.exp(a0 - m), jnp.exp(a1 - m), jnp.exp(a2 - m)
        o = (e0 * o0[...] + e1 * o1[...] + e2 * o2[...]) / (e0 + e1 + e2)
    else:
        oc, os_, ow, gt_ref, w_ref, g_ref, x_ref, out_ref = refs
        gt = gt_ref[...]
        parts = []
        for h in range(NSA_HEADS):
            sl = slice(h * LANES, (h + 1) * LANES)
            parts.append(gt[:, h:h + 1] * oc[:, sl].astype(F32)
                         + gt[:, NSA_HEADS + h:NSA_HEADS + h + 1] * os_[:, sl].astype(F32)
                         + gt[:, 2 * NSA_HEADS + h:2 * NSA_HEADS + h + 1] * ow[:, sl].astype(F32))
        o = jnp.concatenate(parts, axis=1)
    y = jnp.dot(o.astype(BF16), w_ref[...], preferred_element_type=F32)
    out_ref[...] = x_ref[...] + _rms(y, g_ref[...])


def oproj(mode, acts, w, g, x):
    T = x.shape[0]
    tm = _row_tile(T, 512)
    row = lambda a: pl.BlockSpec((tm, a.shape[1]), lambda i: (i, 0))
    full = lambda a: pl.BlockSpec(a.shape, lambda i: (0,) * a.ndim)
    g2 = g.reshape(1, -1)
    est = 2 * (sum(tm * a.shape[1] * a.dtype.itemsize for a in acts) + int(w.size) * 2 + 2 * tm * D_MODEL * 4) + 4 * tm * 2048 * 4
    return pl.pallas_call(
        functools.partial(_oproj_kernel, mode=mode),
        grid=(T // tm,),
        in_specs=[row(a) for a in acts] + [full(w), full(g2), row(x)],
        out_specs=row(x),
        out_shape=jax.ShapeDtypeStruct(x.shape, F32),
        compiler_params=_cparams(("parallel",), est),
        name="oproj_" + mode,
    )(*acts, w, g2, x)


def _mlp_kernel(x_ref, g2_ref, w1_ref, w2_ref, g3_ref, o_ref, xn_scr, acc_scr):
    f = pl.program_id(1)

    @pl.when(f == 0)
    def _():
        xn_scr[...] = _rms(x_ref[...], g2_ref[...]).astype(BF16)
        acc_scr[...] = jnp.zeros(acc_scr.shape, F32)

    h = jnp.dot(xn_scr[...], w1_ref[...], preferred_element_type=F32)
    h = jnp.square(jnp.maximum(h, 0.0)).astype(BF16)
    acc_scr[...] += jnp.dot(h, w2_ref[...], preferred_element_type=F32)

    @pl.when(f == pl.num_programs(1) - 1)
    def _():
        o_ref[...] = x_ref[...] + _rms(acc_scr[...], g3_ref[...])


def mlp(x, g2, w1, w2, g3):
    T = x.shape[0]
    tm = _row_tile(T, 512)
    tf = 1024
    est = 2 * (2 * tm * D_MODEL * 4 + 2 * D_MODEL * tf * 2) + tm * D_MODEL * 6 + 2 * tm * tf * 4
    return pl.pallas_call(
        _mlp_kernel,
        grid=(T // tm, D_FF // tf),
        in_specs=[pl.BlockSpec((tm, D_MODEL), lambda i, f: (i, 0)),
                  pl.BlockSpec((1, D_MODEL), lambda i, f: (0, 0)),
                  pl.BlockSpec((D_MODEL, tf), lambda i, f: (0, f)),
                  pl.BlockSpec((tf, D_MODEL), lambda i, f: (f, 0)),
                  pl.BlockSpec((1, D_MODEL), lambda i, f: (0, 0))],
        out_specs=pl.BlockSpec((tm, D_MODEL), lambda i, f: (i, 0)),
        out_shape=jax.ShapeDtypeStruct(x.shape, F32),
        scratch_shapes=[pltpu.VMEM((tm, D_MODEL), BF16), pltpu.VMEM((tm, D_MODEL), F32)],
        compiler_params=_cparams(("parallel", "arbitrary"), est),
        name="mlp",
    )(x, g2.reshape(1, -1), w1, w2, g3.reshape(1, -1))


def _head_mm_kernel(x_ref, w_ref, o_ref):
    o_ref[...] = jnp.dot(x_ref[...].astype(BF16), w_ref[...], preferred_element_type=F32).astype(o_ref.dtype)


def head_mm(x, w, out_dtype):
    M = x.shape[0]
    H, kb, nb = w.shape
    return pl.pallas_call(
        _head_mm_kernel,
        grid=(H,),
        in_specs=[pl.BlockSpec((M, kb), lambda h: (0, h)), pl.BlockSpec((None, kb, nb), lambda h: (h, 0, 0))],
        out_specs=pl.BlockSpec((M, nb), lambda h: (0, h)),
        out_shape=jax.ShapeDtypeStruct((M, H * nb), out_dtype),
        compiler_params=_cparams(("parallel",), 4 * (M * kb * 4 + kb * nb * 2 + M * nb * 4)),
        name="head_mm",
    )(x, w)


class FlashCfg(NamedTuple):
    R: int
    kv_shared: bool
    tq: int
    tk: int
    window: Optional[int]
    alibi: bool
    dist_scale: float
    slope_stride: int
    slope_base: int
    q_scale: float
    has_sel: bool
    want_lse: bool


def _pair_tables(nq, nk, tq, tk, window):
    qi, kj = [], []
    for i in range(nq):
        q_lo, q_hi = i * tq, i * tq + tq - 1
        for j in range(nk):
            k_lo, k_hi = j * tk, j * tk + tk - 1
            if k_lo <= q_hi and (window is None or k_hi >= q_lo - window):
                qi.append(i)
                kj.append(j)
    n = len(qi)
    first = [1 if (p == 0 or qi[p] != qi[p - 1]) else 0 for p in range(n)]
    last = [1 if (p == n - 1 or qi[p] != qi[p + 1]) else 0 for p in range(n)]
    return tuple(jnp.asarray(np.asarray(a, np.int32)) for a in (qi, kj, first, last))


def _flash_kernel(qi_ref, kj_ref, fi_ref, la_ref, sl_ref, *refs, cfg):
    R, tq, tk = cfg.R, cfg.tq, cfg.tk
    it = iter(refs)
    q_ref, k_ref, v_ref = next(it), next(it), next(it)
    sel_ref = next(it) if cfg.has_sel else None
    e_ref = next(it) if cfg.has_sel else None
    o_ref = next(it)
    lse_ref = next(it) if cfg.want_lse else None
    m_scr, l_scr, acc_scr = next(it), next(it), next(it)
    n2 = pl.program_id(1)
    p = pl.program_id(2)
    qi, kj = qi_ref[p], kj_ref[p]

    @pl.when(fi_ref[p] == 1)
    def _():
        m_scr[...] = jnp.full(m_scr.shape, NEG_INF, F32)
        l_scr[...] = jnp.zeros(l_scr.shape, F32)
        acc_scr[...] = jnp.zeros(acc_scr.shape, F32)

    dist = (qi * tq + lax.broadcasted_iota(I32, (tq, tk), 0)) - (kj * tk + lax.broadcasted_iota(I32, (tq, tk), 1))
    mask = dist >= 0
    if cfg.window is not None:
        mask = mask & (dist <= cfg.window)
    if cfg.has_sel:
        mask = mask & (jnp.dot(sel_ref[...], e_ref[...], preferred_element_type=F32) > 0.5)
    if cfg.alibi:
        distf = dist.astype(F32) * cfg.dist_scale
    for r in range(R):
        sl = slice(r * LANES, (r + 1) * LANES)
        ksl = slice(0, LANES) if cfg.kv_shared else sl
        q = q_ref[:, sl]
        if cfg.q_scale != 1.0 or q.dtype != BF16:
            q = (q.astype(F32) * cfg.q_scale).astype(BF16)
        s = lax.dot_general(q, k_ref[:, ksl].astype(BF16), (((1,), (1,)), ((), ())), preferred_element_type=F32)
        if cfg.alibi:
            s = s - sl_ref[cfg.slope_base + cfg.slope_stride * n2 + r] * distf
        s = jnp.where(mask, s, NEG_INF)
        m_prev = m_scr[r]
        m_new = jnp.maximum(m_prev, jnp.max(s, axis=1, keepdims=True))
        pr = jnp.where(mask, jnp.exp(s - m_new), 0.0)
        alpha = jnp.exp(m_prev - m_new)
        l_scr[r] = alpha * l_scr[r] + jnp.sum(pr, axis=1, keepdims=True)
        acc_scr[r] = alpha * acc_scr[r] + jnp.dot(pr.astype(BF16), v_ref[:, ksl].astype(BF16), preferred_element_type=F32)
        m_scr[r] = m_new

    @pl.when(la_ref[p] == 1)
    def _():
        for r in range(R):
            sl = slice(r * LANES, (r + 1) * LANES)
            l = jnp.maximum(l_scr[r], 1e-30)
            o_ref[:, sl] = (acc_scr[r] / l).astype(o_ref.dtype)
            if cfg.want_lse:
                lse_ref[:, sl] = jnp.broadcast_to(m_scr[r] + jnp.log(l), (tq, LANES))


def flash(cfg, q, k, v, slopes, n1, n2, q_col, k_col, v_col, o_col, o_shape, o_dtype, sel=None, emat=None):
    R, tq, tk = cfg.R, cfg.tq, cfg.tk
    Sq, Sk = q.shape[1], k.shape[1]
    tabs = _pair_tables(Sq // tq, Sk // tk, tq, tk, cfg.window)
    P = int(tabs[0].shape[0])
    kw = LANES if cfg.kv_shared else R * LANES
    in_specs = [pl.BlockSpec((None, tq, R * LANES), lambda a, b, p, qi, kj, fi, la, sl: (a, qi[p], q_col(b))),
                pl.BlockSpec((None, tk, kw), lambda a, b, p, qi, kj, fi, la, sl: (a, kj[p], k_col(b))),
                pl.BlockSpec((None, tk, kw), lambda a, b, p, qi, kj, fi, la, sl: (a, kj[p], v_col(b)))]
    args = [q, k, v]
    if cfg.has_sel:
        in_specs += [pl.BlockSpec((None, None, tq, LANES), lambda a, b, p, qi, kj, fi, la, sl: (a, b, qi[p], 0)),
                     pl.BlockSpec((LANES, tk), lambda a, b, p, qi, kj, fi, la, sl: (0, kj[p]))]
        args += [sel, emat]
    o_spec = pl.BlockSpec((None, tq, R * LANES), lambda a, b, p, qi, kj, fi, la, sl: (a, qi[p], o_col(b)))
    out_specs = [o_spec]
    out_shape = [jax.ShapeDtypeStruct(o_shape, o_dtype)]
    if cfg.want_lse:
        out_specs.append(o_spec)
        out_shape.append(jax.ShapeDtypeStruct(o_shape, F32))
    est = 2 * (tq * R * LANES * q.dtype.itemsize + 2 * tk * kw * k.dtype.itemsize + 2 * tq * R * LANES * 4) \
        + R * tq * LANES * 4 * 3 + 6 * tq * tk * 4
    outs = pl.pallas_call(
        functools.partial(_flash_kernel, cfg=cfg),
        grid_spec=pltpu.PrefetchScalarGridSpec(
            num_scalar_prefetch=5, grid=(n1, n2, P), in_specs=in_specs, out_specs=out_specs,
            scratch_shapes=[pltpu.VMEM((R, tq, 1), F32), pltpu.VMEM((R, tq, 1), F32), pltpu.VMEM((R, tq, LANES), F32)]),
        out_shape=out_shape,
        compiler_params=_cparams(("parallel", "parallel", "arbitrary"), est),
        name="flash",
    )(*tabs, slopes, *args)
    return outs if cfg.want_lse else outs[0]


PAGES_PER_STEP = 16


def _mla_decode_kernel(pt_ref, ql_ref, qr_ref, cn_ref, kn_ref, *refs):
    npg = PAGES_PER_STEP
    lat_refs, rope_refs = refs[:npg], refs[npg:2 * npg]
    o_ref, m_scr, l_scr, acc_scr = refs[2 * npg:]
    j = pl.program_id(1)

    @pl.when(j == 0)
    def _():
        m_scr[...] = jnp.full(m_scr.shape, NEG_INF, F32)
        l_scr[...] = jnp.zeros(l_scr.shape, F32)
        acc_scr[...] = jnp.zeros(acc_scr.shape, F32)

    ql, qr = ql_ref[...], qr_ref[...]
    kl = jnp.concatenate([r[...] for r in lat_refs], axis=0)
    kr = jnp.concatenate([r[...] for r in rope_refs], axis=0)
    nt = (((1,), (1,)), ((), ()))
    s = lax.dot_general(ql, kl, nt, preferred_element_type=F32) + lax.dot_general(qr, kr, nt, preferred_element_type=F32)
    m_prev = m_scr[...]
    m_new = jnp.maximum(m_prev, jnp.max(s, axis=1, keepdims=True))
    pr = jnp.exp(s - m_new)
    alpha = jnp.exp(m_prev - m_new)
    l_scr[...] = alpha * l_scr[...] + jnp.sum(pr, axis=1, keepdims=True)
    acc_scr[...] = alpha * acc_scr[...] + jnp.dot(pr, kl, preferred_element_type=F32)
    m_scr[...] = m_new

    @pl.when(j == pl.num_programs(1) - 1)
    def _():
        cn, kn = cn_ref[...], kn_ref[...]
        s_new = jnp.sum(ql * cn, axis=1, keepdims=True) + jnp.sum(qr * kn, axis=1, keepdims=True)
        m_prev = m_scr[...]
        m_fin = jnp.maximum(m_prev, s_new)
        alpha = jnp.exp(m_prev - m_fin)
        p_new = jnp.exp(s_new - m_fin)
        l = alpha * l_scr[...] + p_new
        o_ref[...] = (alpha * acc_scr[...] + p_new * cn) / l


def mla_decode(page_table, q_lat, q_rope, ckv_new, kr_new, cache_lat, cache_rope):
    DB, NP = page_table.shape
    npg = PAGES_PER_STEP
    assert NP % npg == 0
    H, C, Rr = MLA_HEADS, MLA_KV_LORA, MLA_ROPE
    seq = lambda shp: pl.BlockSpec((None,) + shp, lambda b, j, pt: (b, 0, 0))
    page = lambda i, w: pl.BlockSpec((None, PAGE, w), lambda b, j, pt: (pt[b, j * npg + i], 0, 0))
    in_specs = [seq((H, C)), seq((H, Rr)), seq((1, C)), seq((1, Rr))] \
        + [page(i, C) for i in range(npg)] + [page(i, Rr) for i in range(npg)]
    est = 2 * npg * PAGE * (C + LANES) * 4 + 3 * npg * PAGE * C * 4
    return pl.pallas_call(
        _mla_decode_kernel,
        grid_spec=pltpu.PrefetchScalarGridSpec(
            num_scalar_prefetch=1, grid=(DB, NP // npg), in_specs=in_specs,
            out_specs=pl.BlockSpec((None, H, C), lambda b, j, pt: (b, 0, 0)),
            scratch_shapes=[pltpu.VMEM((H, 1), F32), pltpu.VMEM((H, 1), F32), pltpu.VMEM((H, C), F32)]),
        out_shape=jax.ShapeDtypeStruct((DB, H, C), F32),
        compiler_params=_cparams(("parallel", "arbitrary"), est),
        name="mla_decode",
    )(page_table, q_lat, q_rope, ckv_new, kr_new, *([cache_lat] * npg), *([cache_rope] * npg))


def _dil_decode_kernel(sl_ref, z_ref, c0_ref, c1_ref, c2_ref, o_ref, lse_ref):
    z = z_ref[...]
    nt = (((1,), (1,)), ((), ()))
    row = lax.broadcasted_iota(I32, (1, PAGE), 1)
    for g, (c_ref, (_, dil)) in enumerate(zip((c0_ref, c1_ref, c2_ref), DIL_GROUPS)):
        dist = ((PAGE - row) * dil).astype(F32)
        for h in range(DIL_HEADS):
            base = (g * 3 * DIL_HEADS + h) * LANES
            q = z[:, base:base + LANES] * (DIL_HD ** -0.5)
            kn = z[:, base + DIL_HEADS * LANES: base + (DIL_HEADS + 1) * LANES]
            vn = z[:, base + 2 * DIL_HEADS * LANES: base + (2 * DIL_HEADS + 1) * LANES]
            kc = c_ref[:, h * LANES:(h + 1) * LANES]
            vc = c_ref[:, (DIL_HEADS + h) * LANES:(DIL_HEADS + h + 1) * LANES]
            q8 = jnp.broadcast_to(q, (8, LANES)).astype(BF16)
            s = lax.dot_general(q8, kc.astype(BF16), nt, preferred_element_type=F32)[0:1] \
                - sl_ref[g * DIL_HEADS + h] * dist
            s_new = jnp.sum(q.astype(BF16).astype(F32) * kn.astype(BF16).astype(F32), axis=1, keepdims=True)
            m = jnp.maximum(jnp.max(s, axis=1, keepdims=True), s_new)
            pr = jnp.exp(s - m)
            p_new = jnp.exp(s_new - m)
            l = jnp.sum(pr, axis=1, keepdims=True) + p_new
            pv = jnp.dot(jnp.broadcast_to(pr, (8, PAGE)).astype(BF16), vc.astype(BF16), preferred_element_type=F32)[0:1]
            o_ref[g, :, h * LANES:(h + 1) * LANES] = (pv + p_new * vn) / l
            lse_ref[g, :, h * LANES:(h + 1) * LANES] = jnp.broadcast_to(m + jnp.log(l), (1, LANES))


def dil_decode(slopes, z_dec, caches):
    DB = z_dec.shape[0]
    NG = len(DIL_GROUPS)
    kvw = 2 * DIL_HEADS * DIL_HD
    views = []
    for c, (w, d) in zip(caches, DIL_GROUPS):
        assert c.shape[1] == w, "decode window buffers must be full"
        views.append(c.reshape(DB, w // d, d * kvw))
    out_spec = pl.BlockSpec((NG, None, 1, DIL_HEADS * LANES), lambda b, sl: (0, b, 0, 0))
    shp = jax.ShapeDtypeStruct((NG, DB, 1, DIL_HEADS * LANES), F32)
    return pl.pallas_call(
        _dil_decode_kernel,
        grid_spec=pltpu.PrefetchScalarGridSpec(
            num_scalar_prefetch=1, grid=(DB,),
            in_specs=[pl.BlockSpec((None, 1, z_dec.shape[1]), lambda b, sl: (b, 0, 0))]
                     + [pl.BlockSpec((None, PAGE, kvw), lambda b, sl: (b, 0, 0))] * NG,
            out_specs=[out_spec, out_spec]),
        out_shape=[shp, shp],
        compiler_params=_cparams(("parallel",), 2 * NG * PAGE * kvw * 4 * 2),
        name="dil_decode",
    )(slopes, z_dec.reshape(DB, 1, -1), *views)


SHIFT_CHUNKS = 8


def _shift_kernel(c_ref, n_ref, o_ref, sem):
    DB, W = c_ref.shape[0], c_ref.shape[1]
    nb = DB // SHIFT_CHUNKS if DB % SHIFT_CHUNKS == 0 else DB
    copies = []
    for i, b0 in enumerate(range(0, DB, nb)):
        copies.append(pltpu.make_async_copy(c_ref.at[pl.ds(b0, nb), pl.ds(1, W - 1)],
                                            o_ref.at[pl.ds(b0, nb), pl.ds(0, W - 1)], sem.at[i]))
    copies.append(pltpu.make_async_copy(n_ref, o_ref.at[:, pl.ds(W - 1, 1)], sem.at[len(copies)]))
    for c in copies:
        c.start()
    for c in copies:
        c.wait()


def shift_append(cache, new):
    return pl.pallas_call(
        _shift_kernel,
        in_specs=[pl.BlockSpec(memory_space=pl.ANY), pl.BlockSpec(memory_space=pl.ANY)],
        out_specs=pl.BlockSpec(memory_space=pl.ANY),
        out_shape=jax.ShapeDtypeStruct(cache.shape, cache.dtype),
        scratch_shapes=[pltpu.SemaphoreType.DMA((SHIFT_CHUNKS + 1,))],
        name="shift_append",
    )(cache, new)


NSA_ROW = 2 * NSA_KV_HEADS * NSA_HD
CHUNK_W = CMP_STRIDE * NSA_ROW


def _compress_kernel(*refs, n_blk, blk_rows, has_new, n_out):
    if has_new:
        pt_ref, refs = refs[0], refs[1:]
    a_refs = refs[:n_blk]
    pos = n_blk
    new_ref = None
    if has_new:
        new_ref = refs[pos]
        pos += 1
    pe_ref, w_ref, o_ref, p_scr = refs[pos:pos + 4]
    step = pl.program_id(1)
    rows = n_blk * blk_rows

    def parts(a, row0, nrows):
        for k in range(2):
            ak = jnp.concatenate([a[:, l * NSA_ROW + k * LANES: l * NSA_ROW + (k + 1) * LANES]
                                  for l in range(CMP_STRIDE)], axis=1)
            for half in range(2):
                lhs = (ak + pe_ref[k, half]).astype(BF16)
                p_scr[k, half, pl.ds(row0, nrows), :] = jnp.dot(lhs, w_ref[k, half], preferred_element_type=F32)

    a = a_refs[0][...] if n_blk == 1 else jnp.concatenate([r[...] for r in a_refs], axis=0)
    parts(a, pl.multiple_of(step * rows, 8), rows)

    @pl.when(step == pl.num_programs(1) - 1)
    def _():
        if has_new:
            new_chunk = jnp.concatenate([new_ref[...], jnp.zeros((1, CHUNK_W - NSA_ROW), F32)], axis=1)
            parts(jnp.concatenate([new_chunk, jnp.zeros((7, CHUNK_W), F32)], axis=0), n_out, 8)
        for k in range(2):
            o_ref[k] = jnp.zeros(o_ref.shape[1:], o_ref.dtype)
            o_ref[k, 0:n_out, :] = (p_scr[k, 0, 0:n_out, :] + p_scr[k, 1, 1:n_out + 1, :]).astype(o_ref.dtype)


def nsa_compress_prompt(kv_cmp, pe, w):
    B, S, _ = kv_cmp.shape
    n_ch = S // CMP_STRIDE
    a = kv_cmp.reshape(B, n_ch, CHUNK_W)
    kern = functools.partial(_compress_kernel, n_blk=1, blk_rows=n_ch, has_new=False, n_out=n_ch - 1)
    return pl.pallas_call(
        kern,
        grid=(B, 1),
        in_specs=[pl.BlockSpec((None, n_ch, CHUNK_W), lambda b, s: (b, 0, 0)),
                  pl.BlockSpec(pe.shape, lambda b, s: (0, 0, 0, 0)),
                  pl.BlockSpec(w.shape, lambda b, s: (0, 0, 0, 0))],
        out_specs=pl.BlockSpec((None, 2, n_ch, 2 * LANES), lambda b, s: (b, 0, 0, 0)),
        out_shape=jax.ShapeDtypeStruct((B, 2, n_ch, 2 * LANES), BF16),
        scratch_shapes=[pltpu.VMEM((2, 2, n_ch, 2 * LANES), F32)],
        compiler_params=_cparams(("parallel", "arbitrary"), 3 * n_ch * CHUNK_W * 4 + 2 * int(w.size) * 2),
        name="nsa_compress_prompt",
    )(a, pe, w)


def nsa_compress_decode(page_table, pool, new_row, pe, w):
    DB, NP = page_table.shape
    npg = PAGES_PER_STEP
    assert NP % npg == 0
    cpp = PAGE // CMP_STRIDE
    n_ch = NP * cpp
    pool_v = pool.reshape(pool.shape[0], cpp, CHUNK_W)
    kern = functools.partial(_compress_kernel, n_blk=npg, blk_rows=cpp, has_new=True, n_out=n_ch)
    page = lambda i: pl.BlockSpec((None, cpp, CHUNK_W), lambda b, s, pt: (pt[b, s * npg + i], 0, 0))
    return pl.pallas_call(
        kern,
        grid_spec=pltpu.PrefetchScalarGridSpec(
            num_scalar_prefetch=1, grid=(DB, NP // npg),
            in_specs=[page(i) for i in range(npg)]
                     + [pl.BlockSpec((None, 1, NSA_ROW), lambda b, s, pt: (b, 0, 0)),
                        pl.BlockSpec(pe.shape, lambda b, s, pt: (0, 0, 0, 0)),
                        pl.BlockSpec(w.shape, lambda b, s, pt: (0, 0, 0, 0))],
            out_specs=pl.BlockSpec((None, 2, n_ch, 2 * LANES), lambda b, s, pt: (b, 0, 0, 0)),
            scratch_shapes=[pltpu.VMEM((2, 2, n_ch + 8, 2 * LANES), F32)]),
        out_shape=jax.ShapeDtypeStruct((DB, 2, n_ch, 2 * LANES), BF16),
        compiler_params=_cparams(("parallel", "arbitrary"),
                                 4 * npg * cpp * CHUNK_W * 4 + 2 * int(w.size) * 2 + 8 * n_ch * 256 * 4),
        name="nsa_compress_decode",
    )(page_table, *([pool_v] * npg), new_row, pe, w)


def _cmp_attn_kernel(sl_ref, q_ref, ck_ref, cv_ref, ov_ref, oc_ref, sel_ref, idx_ref, *, tq, n_cmp, n_sel, q_off, stacked):
    g = pl.program_id(1)
    qi = pl.program_id(2)
    G = NSA_GROUP
    if stacked:
        qs = q_ref[...]
    else:
        qs = jnp.concatenate([q_ref[:, r * LANES:(r + 1) * LANES] for r in range(G)], axis=0)
    nc = ck_ref.shape[0]
    rows = G * tq
    s = lax.dot_general(qs, ck_ref[...], (((1,), (1,)), ((), ())), preferred_element_type=F32)
    ridx = lax.broadcasted_iota(I32, (rows, 1), 0)
    qpos_r = q_off + qi * tq + (ridx & (tq - 1))
    slope = jnp.zeros((rows, 1), F32)
    for r in range(G):
        slope = jnp.where((ridx >= r * tq) & (ridx < (r + 1) * tq), sl_ref[g * G + r], slope)
    n_iota = lax.broadcasted_iota(I32, (1, nc), 1)
    dist = qpos_r - (n_iota * CMP_STRIDE + CMP_LEN - 1)
    valid = (dist >= 0) & (n_iota < n_cmp)
    s = jnp.where(valid, s - slope * dist.astype(F32), NEG_INF)
    m = jnp.max(s, axis=1, keepdims=True)
    e = jnp.where(valid, jnp.exp(s - m), 0.0)
    pc = e / jnp.maximum(jnp.sum(e, axis=1, keepdims=True), 1e-30)
    oc = jnp.dot(pc.astype(BF16), cv_ref[...], preferred_element_type=F32)
    if stacked:
        oc_ref[...] = oc.astype(oc_ref.dtype)
        psum = jnp.sum(pc, axis=0, keepdims=True)
    else:
        for r in range(G):
            oc_ref[:, r * LANES:(r + 1) * LANES] = oc[r * tq:(r + 1) * tq].astype(oc_ref.dtype)
        psum = pc[0:tq]
        for r in range(1, G):
            psum = psum + pc[r * tq:(r + 1) * tq]
    hi = psum.astype(BF16)
    lo = (psum - hi.astype(F32)).astype(BF16)
    ov = ov_ref[...]
    imp = jnp.dot(hi, ov, preferred_element_type=F32) + jnp.dot(lo, ov, preferred_element_type=F32)
    ns = imp.shape[1]
    blk = lax.broadcasted_iota(I32, (tq, ns), 1)
    qpos = q_off + qi * tq + lax.broadcasted_iota(I32, (tq, ns), 0)
    cur = lax.shift_right_arithmetic(qpos, int(math.log2(SEL_LEN)))
    forced = (blk == 0) | (blk == cur) | (blk == cur - 1)
    score = jnp.where(forced, imp + FORCE_BONUS, imp)
    score = jnp.where(blk * SEL_LEN <= qpos, score, -1.0)
    score = jnp.where(blk < n_sel, score, -2.0)
    blkf = blk.astype(F32)
    chosen = jnp.zeros((tq, ns), F32)
    idx = jnp.zeros((tq, LANES), F32)
    lane = lax.broadcasted_iota(I32, (tq, LANES), 1)
    for t in range(SEL_TOP):
        mx = jnp.max(score, axis=1, keepdims=True)
        first = jnp.min(jnp.where(score == mx, blkf, float(ns)), axis=1, keepdims=True)
        hit = blkf == first
        chosen = jnp.where(hit, 1.0, chosen)
        score = jnp.where(hit, -3.0, score)
        idx = jnp.where(lane == t, first, idx)
    sel_ref[...] = chosen.astype(sel_ref.dtype)
    idx_ref[...] = idx.astype(I32)


def nsa_cmp_attn(slopes, q, ckv, ov, *, tq, n_cmp, n_sel, q_off, stacked):
    NB = q.shape[0]
    nc = ckv.shape[2]
    ns = ov.shape[1]
    G = NSA_GROUP
    if stacked:
        nq = 1
        q_spec = pl.BlockSpec((None, None, G, LANES), lambda b, g, i, sl: (b, g, 0, 0))
        oc_spec, oc_shape = q_spec, (NB, NSA_KV_HEADS, G, LANES)
    else:
        nq = q.shape[1] // tq
        q_spec = pl.BlockSpec((None, tq, G * LANES), lambda b, g, i, sl: (b, i, g))
        oc_spec, oc_shape = q_spec, q.shape
    sq = nq * tq
    kern = functools.partial(_cmp_attn_kernel, tq=tq, n_cmp=n_cmp, n_sel=n_sel, q_off=q_off, stacked=stacked)
    return pl.pallas_call(
        kern,
        grid_spec=pltpu.PrefetchScalarGridSpec(
            num_scalar_prefetch=1, grid=(NB, NSA_KV_HEADS, nq),
            in_specs=[q_spec,
                      pl.BlockSpec((None, None, nc, LANES), lambda b, g, i, sl: (b, 0, 0, g)),
                      pl.BlockSpec((None, None, nc, LANES), lambda b, g, i, sl: (b, 1, 0, g)),
                      pl.BlockSpec(ov.shape, lambda b, g, i, sl: (0, 0))],
            out_specs=[oc_spec,
                       pl.BlockSpec((None, None, tq, ns), lambda b, g, i, sl: (b, g, i, 0)),
                       pl.BlockSpec((None, None, tq, LANES), lambda b, g, i, sl: (b, g, i, 0))]),
        out_shape=[jax.ShapeDtypeStruct(oc_shape, BF16),
                   jax.ShapeDtypeStruct((NB, NSA_KV_HEADS, sq, ns), BF16),
                   jax.ShapeDtypeStruct((NB, NSA_KV_HEADS, sq, LANES), I32)],
        compiler_params=_cparams(("parallel", "parallel", "parallel"), 16 * G * tq * nc * 4),
        name="nsa_cmp_attn",
    )(slopes, q, ckv, ckv, ov)


N_SEL_SLOTS = NSA_KV_HEADS * SEL_TOP


def _nsa_decode_kernel(pt_ref, ix_ref, sl_ref, q_ref, sn_ref, wn_ref, win_ref, *refs, past):
    blk_refs = refs[:N_SEL_SLOTS]
    os_ref, ow_ref = refs[N_SEL_SLOTS:]
    b = pl.program_id(0)
    G = NSA_GROUP
    nt = (((1,), (1,)), ((), ()))
    ridx = lax.broadcasted_iota(I32, (G, 1), 0)
    lane = lax.broadcasted_iota(I32, (G, LANES), 1)
    n_last = past // SEL_LEN
    for g in range(NSA_KV_HEADS):
        q = q_ref[g].astype(F32)
        qk = q if g == 0 else pltpu.roll(q, NSA_HD, 1)
        qk_b = qk.astype(BF16)
        slope = jnp.zeros((G, 1), F32)
        for r in range(G):
            slope = jnp.where(ridx == r, sl_ref[g * G + r], slope)

        def branch(kv, dist, valid, new_row):
            s = lax.dot_general(qk_b, kv[:, 0:LANES].astype(BF16), nt, preferred_element_type=F32)
            s = jnp.where(valid, s - slope * dist, NEG_INF)
            kn = new_row[:, 0:LANES].astype(BF16).astype(F32)
            s_new = jnp.sum(qk_b.astype(F32) * kn, axis=1, keepdims=True)
            m = jnp.maximum(jnp.max(s, axis=1, keepdims=True), s_new)
            pr = jnp.where(valid, jnp.exp(s - m), 0.0)
            p_new = jnp.exp(s_new - m)
            l = jnp.sum(pr, axis=1, keepdims=True) + p_new
            o = (jnp.dot(pr.astype(BF16), kv[:, LANES:2 * LANES].astype(BF16), preferred_element_type=F32)
                 + p_new * new_row[:, LANES:2 * LANES]) / l
            o = o if g == 0 else pltpu.roll(o, NSA_HD, 1)
            return jnp.where(lane < NSA_HD, o, 0.0)

        kv = jnp.concatenate([blk_refs[g * SEL_TOP + j][...] for j in range(SEL_TOP)], axis=0)
        nk = SEL_TOP * SEL_LEN
        kidx = lax.broadcasted_iota(I32, (1, nk), 1)
        slot = lax.shift_right_arithmetic(kidx, int(math.log2(SEL_LEN)))
        kpos = jnp.zeros((1, nk), I32)
        ok = jnp.zeros((1, nk), jnp.bool_)
        for j in range(SEL_TOP):
            bj = ix_ref[b, g * SEL_TOP + j]
            kpos = jnp.where(slot == j, bj * SEL_LEN + (kidx - j * SEL_LEN), kpos)
            ok = ok | ((slot == j) & (bj < n_last))
        os_ref[g] = branch(kv, (past - kpos).astype(F32), ok, sn_ref[...]).astype(os_ref.dtype)
        wrows = win_ref.shape[0]
        wd = (wrows - lax.broadcasted_iota(I32, (1, wrows), 1)).astype(F32)
        ow_ref[g] = branch(win_ref[...], wd, wd > 0.0, wn_ref[...]).astype(ow_ref.dtype)


def nsa_decode(page_table, sel_idx, slopes, q, sel_new, win_new, cache_win, pool_sel, past):
    DB, NP = page_table.shape
    hp = PAGE // SEL_LEN
    pool_v = pool_sel.reshape(pool_sel.shape[0], hp, SEL_LEN, NSA_ROW)
    last_cached = NP * hp - 1

    def blk_spec(slot):
        def imap(b, pt, ix, sl):
            bj = jnp.minimum(ix[b, slot], last_cached)
            return (pt[b, bj // hp], bj % hp, 0, 0)
        return pl.BlockSpec((None, None, SEL_LEN, NSA_ROW), imap)

    q_spec = pl.BlockSpec((None, NSA_KV_HEADS, NSA_GROUP, LANES), lambda b, pt, ix, sl: (b, 0, 0, 0))
    row_spec = pl.BlockSpec((None, 1, NSA_ROW), lambda b, pt, ix, sl: (b, 0, 0))
    wrows = cache_win.shape[1]
    shp = jax.ShapeDtypeStruct((DB, NSA_KV_HEADS, NSA_GROUP, LANES), BF16)
    return pl.pallas_call(
        functools.partial(_nsa_decode_kernel, past=past),
        grid_spec=pltpu.PrefetchScalarGridSpec(
            num_scalar_prefetch=3, grid=(DB,),
            in_specs=[q_spec, row_spec, row_spec,
                      pl.BlockSpec((None, wrows, NSA_ROW), lambda b, pt, ix, sl: (b, 0, 0))]
                     + [blk_spec(s) for s in range(N_SEL_SLOTS)],
            out_specs=[q_spec, q_spec]),
        out_shape=[shp, shp],
        compiler_params=_cparams(("parallel",), 4 * (wrows + N_SEL_SLOTS * SEL_LEN) * NSA_ROW * 4),
        name="nsa_decode",
    )(page_table, sel_idx, slopes, q, sel_new, win_new, cache_win, *([pool_v] * N_SEL_SLOTS))


def _alibi_slopes(n):
    return 2.0 ** (-8.0 * jnp.arange(1, n + 1, dtype=F32) / n)


def _pad_last(a, n):
    return jnp.concatenate([a, jnp.zeros(a.shape[:-1] + (n - a.shape[-1],), a.dtype)], axis=-1)


def _rope_tables(pos, q_scale):
    half = MLA_ROPE // 2
    inv = ROPE_THETA ** (-jnp.arange(half, dtype=F32) / half)
    ang = pos.astype(F32)[:, None] * inv[None, :]
    cos, sin = jnp.cos(ang), jnp.sin(ang)
    n = pos.shape[0]
    one, z32 = jnp.ones((n, MLA_NOPE), F32), jnp.zeros((n, LANES - MLA_NOPE - MLA_ROPE), F32)
    c = jnp.concatenate([one, cos, cos, z32], axis=1)
    s = jnp.concatenate([0.0 * one, -sin, sin, z32], axis=1)
    return (q_scale * c, q_scale * s, c, s)


def _prep_mla(w_in, q_norm, w_qb, kv_norm, w_kvb, w_o):
    ql, kl, rp, half = MLA_Q_LORA, MLA_KV_LORA, MLA_ROPE, MLA_ROPE // 2
    wr = w_in[:, ql + kl:]
    z64 = jnp.zeros((D_MODEL, MLA_NOPE), F32)
    z32 = jnp.zeros((D_MODEL, LANES - MLA_NOPE - rp), F32)
    qb = w_qb.reshape(ql, MLA_HEADS, MLA_NOPE + rp)
    nope, rope = qb[..., :MLA_NOPE], qb[..., MLA_NOPE:]
    zq = jnp.zeros((ql, MLA_HEADS, MLA_NOPE), F32)
    uk, uv = w_kvb[..., :MLA_NOPE], w_kvb[..., MLA_NOPE:]
    bf = lambda a: a.astype(BF16)
    return dict(
        wq=bf(w_in[:, :ql]), wkv=bf(w_in[:, ql:ql + kl]),
        wr=bf(jnp.concatenate([z64, wr, z32], axis=1)),
        wrs=bf(jnp.concatenate([z64, wr[:, half:], wr[:, :half], z32], axis=1)),
        qn=q_norm.reshape(1, -1), kvn=kv_norm.reshape(1, -1),
        wqb=bf(_pad_last(jnp.concatenate([nope, rope], axis=-1), LANES).reshape(ql, -1)),
        wqbs=bf(_pad_last(jnp.concatenate([zq, rope[..., half:], rope[..., :half]], axis=-1), LANES).reshape(ql, -1)),
        wuk=bf(_pad_last(uk, LANES).reshape(kl, -1)),
        wuv=bf(_pad_last(uv, LANES).reshape(kl, -1)),
        wukT=bf(jnp.concatenate([jnp.transpose(uk, (1, 2, 0)), jnp.zeros((MLA_HEADS, LANES - MLA_NOPE, kl), F32)], axis=1)),
        wuvH=bf(_pad_last(jnp.transpose(uv, (1, 0, 2)), LANES)),
        wo=bf(_pad_last(w_o.reshape(MLA_HEADS, MLA_V, D_MODEL).transpose(0, 2, 1), LANES).transpose(0, 2, 1).reshape(-1, D_MODEL)),
    )


def _prep_nsa(w_in, cmp_w, cmp_pe, w_o):
    H, D, G = NSA_HEADS, NSA_HD, NSA_KV_HEADS
    qw = H * D
    kvw = 3 * 2 * G * D
    bf = lambda a: a.astype(BF16)
    wq = _pad_last((w_in[:, :qw] * (D ** -0.5)).reshape(D_MODEL, H, D), LANES).reshape(D_MODEL, -1)
    wkv = w_in[:, qw:qw + kvw]
    wkvp = _pad_last(wkv[:, 2 * G * D:].reshape(D_MODEL, 2 * 2 * G, D), LANES).reshape(D_MODEL, -1)
    wg = _pad_last(w_in[:, qw + kvw:], LANES)
    r = CMP_LEN // CMP_STRIDE
    base = cmp_w.reshape(2, r, CMP_STRIDE, D, D)
    wc = jnp.zeros((2, r, CMP_STRIDE, G, D, G, LANES), F32)
    for g in range(G):
        wc = wc.at[:, :, :, g, :, g, :D].set(base)
    pe = jnp.broadcast_to(cmp_pe.reshape(2, r, CMP_STRIDE, 1, D), (2, r, CMP_STRIDE, G, D))
    return dict(
        wq=bf(wq), wkv=bf(wkv), wkvp=bf(wkvp), wg=bf(wg),
        wc=bf(wc.reshape(2, r, CMP_STRIDE * G * D, G * LANES)),
        pe=pe.reshape(2, r, 1, CMP_STRIDE * G * D),
        wo=bf(_pad_last(w_o.reshape(H, D, D_MODEL).transpose(0, 2, 1), LANES).transpose(0, 2, 1).reshape(-1, D_MODEL)),
    )


def _overlap(nc, n_cmp, ns, n_sel):
    c0 = np.arange(nc)[:, None] * CMP_STRIDE
    s0 = np.arange(ns)[None, :] * SEL_LEN
    ov = np.maximum(np.minimum(c0 + CMP_LEN, s0 + SEL_LEN) - np.maximum(c0, s0), 0) / CMP_STRIDE
    ov = ov * (np.arange(nc)[:, None] < n_cmp) * (np.arange(ns)[None, :] < n_sel)
    return jnp.asarray(ov.astype(np.float32)).astype(BF16)


def _mla_layer(xp, xs, g0, g1, w, tabs_p, tabs_s, cache_lat, cache_rope, page_table, B, S):
    DB = xs.shape[0]
    HP = MLA_HEADS * LANES
    ckv_p, kr_p, q_p, k_p, v_p = mla_prep(xp, g0, w, tabs_p)
    ckv_s, kr_s, q_s, _, _ = mla_prep(xs, g0, w, tabs_s)
    R = 2
    t = _row_tile(S, 512)
    cfg = FlashCfg(R=R, kv_shared=False, tq=t, tk=t, window=None, alibi=False, dist_scale=1.0,
                   slope_stride=0, slope_base=0, q_scale=1.0, has_sel=False, want_lse=False)
    col = lambda h: h
    o_p = flash(cfg, q_p.reshape(B, S, HP), k_p.reshape(B, S, HP), v_p.reshape(B, S, HP), jnp.zeros((1,), F32),
                B, MLA_HEADS // R, col, col, col, col, (B, S, HP), BF16)
    q_lat = head_mm(q_s, w["wukT"], BF16).reshape(DB, MLA_HEADS, MLA_KV_LORA)
    q_rope = q_s.reshape(DB, MLA_HEADS, LANES)[:, :, MLA_NOPE:MLA_NOPE + MLA_ROPE]
    kr_s32 = kr_s[:, MLA_NOPE:MLA_NOPE + MLA_ROPE]
    o_lat = mla_decode(page_table, q_lat, q_rope, ckv_s.reshape(DB, 1, -1), kr_s32.reshape(DB, 1, -1), cache_lat, cache_rope)
    o_s = head_mm(o_lat.reshape(DB, -1), w["wuvH"], BF16)
    xp = oproj("plain", [o_p.reshape(B * S, HP)], w["wo"], g1, xp)
    xs = oproj("plain", [o_s], w["wo"], g1, xs)
    state = (ckv_p.reshape(B, S, -1), kr_p[:, MLA_NOPE:MLA_NOPE + MLA_ROPE].reshape(B, S, -1),
             ckv_s.reshape(DB, 1, -1), kr_s32.reshape(DB, 1, -1))
    return xp, xs, state


def _dil_layer(xp, xs, g0, g1, w_in, w_o, caches, B, S):
    DB = xs.shape[0]
    NG = len(DIL_GROUPS)
    zw = NG * 3 * DIL_HEADS * DIL_HD
    hw = DIL_HEADS * DIL_HD
    slopes = _alibi_slopes(NG * DIL_HEADS)
    w_in_b = w_in.astype(BF16)
    z_p = rms_proj(xp, g0, w_in_b, zw // 2)
    z_s = rms_proj(xs, g0, w_in_b, zw // 2)
    acts_o, acts_l = [], []
    for gi, (win, d) in enumerate(DIL_GROUPS):
        L = S // d
        assert win // d == LANES and L % LANES == 0
        zv = z_p.reshape(B, L, d * zw)
        cfg = FlashCfg(R=DIL_HEADS, kv_shared=False, tq=LANES, tk=LANES, window=win // d, alibi=True, dist_scale=float(d),
                       slope_stride=0, slope_base=gi * DIL_HEADS, q_scale=DIL_HD ** -0.5, has_sel=False, want_lse=True)
        upb = zw // hw
        o, lse = flash(cfg, zv, zv, zv, slopes, B, d,
                       lambda r, gi=gi: r * upb + gi * 3, lambda r, gi=gi: r * upb + gi * 3 + 1,
                       lambda r, gi=gi: r * upb + gi * 3 + 2, lambda r: r, (B, L, d * hw), F32)
        acts_o.append(o.reshape(B * S, hw))
        acts_l.append(lse.reshape(B * S, hw))
    o_d, lse_d = dil_decode(slopes, z_s, [c.reshape(DB, c.shape[1], -1) for c in caches])
    w_o_b = w_o.astype(BF16)
    xp = oproj("dil", acts_o + acts_l, w_o_b, g1, xp)
    xs = oproj("dil", [o_d[i].reshape(DB, hw) for i in range(NG)] + [lse_d[i].reshape(DB, hw) for i in range(NG)], w_o_b, g1, xs)
    z6 = z_p.reshape(B, S, NG, 3, DIL_HEADS, DIL_HD)
    st_p = [z6[:, S - min(win, S):, gi, 1:] for gi, (win, _) in enumerate(DIL_GROUPS)]
    zs6 = z_s.reshape(DB, NG, 3, DIL_HEADS * DIL_HD)
    st_s = []
    for gi, c in enumerate(caches):
        new = zs6[:, gi, 1:].reshape(DB, 1, 2, DIL_HEADS, DIL_HD)
        st_s.append(shift_append(c, new))
    return xp, xs, st_p, st_s


def _nsa_layer(xp, xs, g0, g1, w, cache_cmp, cache_sel, cache_win, page_table, B, S):
    DB, NP = page_table.shape
    past = NP * PAGE
    HP = NSA_HEADS * LANES
    slopes = _alibi_slopes(NSA_HEADS)
    q_p, cmp_p, sel_p, win_p, kvp_p, gt_p = nsa_proj(xp, g0, w)
    q_s, cmp_s, sel_s, win_s, _, gt_s = nsa_proj(xs, g0, w)
    n_ch = S // CMP_STRIDE
    ckv_p = nsa_compress_prompt(cmp_p.reshape(B, S, NSA_ROW), w["pe"], w["wc"])
    q3 = q_p.reshape(B, S, HP)
    oc_p, selmask, _ = nsa_cmp_attn(slopes, q3, ckv_p, _overlap(n_ch, n_ch - 1, LANES, S // SEL_LEN),
                                    tq=LANES, n_cmp=n_ch - 1, n_sel=S // SEL_LEN, q_off=0, stacked=False)
    kv3 = kvp_p.reshape(B, S, 8 * LANES)
    emat = jnp.asarray((np.arange(S)[None, :] // SEL_LEN == np.arange(LANES)[:, None]).astype(np.float32)).astype(BF16)
    base = dict(R=NSA_GROUP, kv_shared=True, alibi=True, dist_scale=1.0, slope_stride=NSA_GROUP, slope_base=0,
                q_scale=1.0, want_lse=False)
    cfg_s = FlashCfg(tq=256, tk=512, window=None, has_sel=True, **base)
    os_p = flash(cfg_s, q3, kv3, kv3, slopes, B, NSA_KV_HEADS, lambda g: g, lambda g: g, lambda g: 2 + g, lambda g: g,
                 (B, S, HP), BF16, sel=selmask, emat=emat)
    cfg_w = FlashCfg(tq=256, tk=256, window=NSA_WIN, has_sel=False, **base)
    ow_p = flash(cfg_w, q3, kv3, kv3, slopes, B, NSA_KV_HEADS, lambda g: g, lambda g: 4 + g, lambda g: 6 + g, lambda g: g,
                 (B, S, HP), BF16)
    xp = oproj("nsa", [oc_p.reshape(B * S, HP), os_p.reshape(B * S, HP), ow_p.reshape(B * S, HP), gt_p], w["wo"], g1, xp)
    pool_cmp = cache_cmp.reshape(cache_cmp.shape[0], PAGE, NSA_ROW)
    pool_sel = cache_sel.reshape(cache_sel.shape[0], PAGE, NSA_ROW)
    cwin = cache_win.reshape(DB, cache_win.shape[1], NSA_ROW)
    assert cwin.shape[1] == NSA_WIN
    ckv_s = nsa_compress_decode(page_table, pool_cmp, cmp_s.reshape(DB, 1, NSA_ROW), w["pe"], w["wc"])
    nc_s = NP * (PAGE // CMP_STRIDE)
    n_sel_s = past // SEL_LEN + 1
    q4 = q_s.reshape(DB, NSA_KV_HEADS, NSA_GROUP, LANES)
    oc_s, _, idx = nsa_cmp_attn(slopes, q4, ckv_s, _overlap(nc_s, nc_s, 2 * LANES, n_sel_s),
                                tq=1, n_cmp=nc_s, n_sel=n_sel_s, q_off=past, stacked=True)
    sel_idx = idx[:, :, 0, :SEL_TOP].reshape(DB, N_SEL_SLOTS)
    os_s, ow_s = nsa_decode(page_table, sel_idx, slopes, q4, sel_s.reshape(DB, 1, NSA_ROW), win_s.reshape(DB, 1, NSA_ROW),
                            cwin, pool_sel, past)
    xs = oproj("nsa", [oc_s.reshape(DB, HP), os_s.reshape(DB, HP), ow_s.reshape(DB, HP), gt_s], w["wo"], g1, xs)
    kvshape = (2, NSA_KV_HEADS, NSA_HD)
    st_p = (cmp_p.reshape(B, S, *kvshape), sel_p.reshape(B, S, *kvshape),
            win_p.reshape(B, S, *kvshape)[:, S - min(NSA_WIN, S):])
    win_new = shift_append(cache_win, win_s.reshape(DB, 1, *kvshape))
    st_s = (cmp_s.reshape(DB, 1, *kvshape), sel_s.reshape(DB, 1, *kvshape), win_new)
    return xp, xs, st_p, st_s


def kernel(x_prompt, x_sample, cache_mla_latent, cache_mla_rope, cache_dil_kv0, cache_dil_kv1, cache_dil_kv2,
           cache_nsa_cmp, cache_nsa_sel, cache_nsa_win, page_table, norm_gains, mlp_w1, mlp_w2, mla_w_in,
           mla_q_norm, mla_w_qb, mla_kv_norm, mla_w_kvb, mla_w_o, dil_w_in, dil_w_o, nsa_w_in, nsa_cmp_w,
           nsa_cmp_pe, nsa_w_o):
    B, S, _ = x_prompt.shape
    DB, n_s, _ = x_sample.shape
    assert n_s == 1
    NP = page_table.shape[1]
    past = NP * PAGE
    depth = norm_gains.shape[0]
    xp = x_prompt.reshape(B * S, D_MODEL)
    xs = x_sample.reshape(DB, D_MODEL)
    mla_scale = (MLA_NOPE + MLA_ROPE) ** -0.5
    tabs_p = _rope_tables(jnp.tile(jnp.arange(S, dtype=I32), B), mla_scale)
    tabs_s = _rope_tables(jnp.full((DB,), past, I32), mla_scale)
    mla_st = [[] for _ in range(4)]
    dil_st = [[] for _ in range(6)]
    nsa_st = [[] for _ in range(6)]
    ia = ib = ic = 0
    for i in range(depth):
        g = norm_gains[i]
        kind = i % 3
        if kind == 0:
            w = _prep_mla(mla_w_in[ia], mla_q_norm[ia], mla_w_qb[ia], mla_kv_norm[ia], mla_w_kvb[ia], mla_w_o[ia])
            xp, xs, st = _mla_layer(xp, xs, g[0], g[1], w, tabs_p, tabs_s, cache_mla_latent[ia], cache_mla_rope[ia],
                                    page_table, B, S)
            for lst, a in zip(mla_st, st):
                lst.append(a)
            ia += 1
        elif kind == 1:
            xp, xs, st_p, st_s = _dil_layer(xp, xs, g[0], g[1], dil_w_in[ib], dil_w_o[ib],
                                            (cache_dil_kv0[ib], cache_dil_kv1[ib], cache_dil_kv2[ib]), B, S)
            for lst, a in zip(dil_st, (*st_p, *st_s)):
                lst.append(a)
            ib += 1
        else:
            w = _prep_nsa(nsa_w_in[ic], nsa_cmp_w[ic], nsa_cmp_pe[ic], nsa_w_o[ic])
            xp, xs, st_p, st_s = _nsa_layer(xp, xs, g[0], g[1], w, cache_nsa_cmp[ic], cache_nsa_sel[ic],
                                            cache_nsa_win[ic], page_table, B, S)
            for lst, a in zip(nsa_st, (*st_p, *st_s)):
                lst.append(a)
            ic += 1
        w1, w2 = mlp_w1[i].astype(BF16), mlp_w2[i].astype(BF16)
        xp = mlp(xp, g[2], w1, w2, g[3])
        xs = mlp(xs, g[2], w1, w2, g[3])
    stack = lambda lst: jnp.stack(lst)
    return (xp.reshape(B, S, D_MODEL), xs.reshape(DB, 1, D_MODEL),
            *[stack(l) for l in mla_st], *[stack(l) for l in dil_st], *[stack(l) for l in nsa_st])
```

```python
import functools
import math
from typing import NamedTuple, Optional

import numpy as np
import jax
import jax.numpy as jnp
from jax import lax
from jax.experimental import pallas as pl
from jax.experimental.pallas import tpu as pltpu

F32 = jnp.float32
BF16 = jnp.bfloat16
I32 = jnp.int32

D_MODEL = 1024
D_FF = 4 * D_MODEL
RMS_EPS = 1e-6
NEG_INF = -1e30
PAGE = 128
MLA_HEADS, MLA_NOPE, MLA_ROPE, MLA_V = 16, 64, 32, 64
MLA_Q_LORA, MLA_KV_LORA = D_MODEL // 2, D_MODEL // 4
ROPE_THETA = 10000.0
DIL_GROUPS = ((128, 1), (512, 4), (2048, 16))
DIL_HEADS, DIL_HD = 4, 128
NSA_HEADS, NSA_KV_HEADS, NSA_HD = 16, 2, 64
NSA_GROUP = NSA_HEADS // NSA_KV_HEADS
CMP_LEN, CMP_STRIDE, SEL_LEN, SEL_TOP, NSA_WIN = 32, 16, 64, 16, 512
FORCE_BONUS = 1e4
LANES = 128
V7X_VMEM_CAP = 60 << 20


def _cparams(sem, vmem_bytes):
    lim = int(min(max(vmem_bytes * 5 // 4 + (4 << 20), 32 << 20), V7X_VMEM_CAP))
    return pltpu.CompilerParams(dimension_semantics=sem, vmem_limit_bytes=lim)


def _row_tile(n, pref):
    t = min(n, pref)
    while n % t:
        t //= 2
    return t


def _rms(x, g):
    ms = jnp.mean(x * x, axis=-1, keepdims=True)
    return (x * lax.rsqrt(ms + RMS_EPS)) * g


def _rms_proj_kernel(x_ref, g_ref, w_ref, o_ref):
    xn = _rms(x_ref[...], g_ref[...]).astype(BF16)
    o_ref[...] = jnp.dot(xn, w_ref[...], preferred_element_type=F32).astype(o_ref.dtype)


def rms_proj(x, g, w, tn, out_dtype=F32):
    T, K = x.shape
    N = w.shape[1]
    tm = _row_tile(T, 512)
    est = 2 * (tm * K * 4 + K * tn * 2 + tm * tn * 4)
    return pl.pallas_call(
        _rms_proj_kernel,
        grid=(T // tm, N // tn),
        in_specs=[pl.BlockSpec((tm, K), lambda i, j: (i, 0)),
                  pl.BlockSpec((1, K), lambda i, j: (0, 0)),
                  pl.BlockSpec((K, tn), lambda i, j: (0, j))],
        out_specs=pl.BlockSpec((tm, tn), lambda i, j: (i, j)),
        out_shape=jax.ShapeDtypeStruct((T, N), out_dtype),
        compiler_params=_cparams(("parallel", "parallel"), est),
        name="rms_proj",
    )(x, g.reshape(1, K), w)


def _mla_prep_kernel(x_ref, g_ref, wq_ref, wkv_ref, wr_ref, wrs_ref, qn_ref, kvn_ref,
                     wqb_ref, wqbs_ref, wuk_ref, wuv_ref, cq_ref, sq_ref, ck_ref, sk_ref,
                     ckv_ref, kr_ref, q_ref, k_ref, v_ref):
    xn = _rms(x_ref[...], g_ref[...]).astype(BF16)
    dot = functools.partial(jnp.dot, preferred_element_type=F32)
    cq = _rms(dot(xn, wq_ref[...]), qn_ref[...]).astype(BF16)
    ckv = _rms(dot(xn, wkv_ref[...]), kvn_ref[...])
    ckv_ref[...] = ckv
    ckv_b = ckv.astype(BF16)
    kr = dot(xn, wr_ref[...]) * ck_ref[...] + dot(xn, wrs_ref[...]) * sk_ref[...]
    kr_ref[...] = kr
    q = dot(cq, wqb_ref[...])
    qs = dot(cq, wqbs_ref[...])
    kn = dot(ckv_b, wuk_ref[...])
    cqt, sqt = cq_ref[...], sq_ref[...]
    for h in range(MLA_HEADS):
        sl = slice(h * LANES, (h + 1) * LANES)
        q_ref[:, sl] = (q[:, sl] * cqt + qs[:, sl] * sqt).astype(BF16)
        k_ref[:, sl] = (kn[:, sl] + kr).astype(BF16)
    v_ref[...] = dot(ckv_b, wuv_ref[...]).astype(BF16)


def mla_prep(x, g, w, tabs):
    T = x.shape[0]
    tm = _row_tile(T, 512)
    HP = MLA_HEADS * LANES
    row = lambda n: pl.BlockSpec((tm, n), lambda i: (i, 0))
    full = lambda a: pl.BlockSpec(a.shape, lambda i: (0,) * a.ndim)
    weights = [g.reshape(1, -1), w["wq"], w["wkv"], w["wr"], w["wrs"], w["qn"], w["kvn"],
               w["wqb"], w["wqbs"], w["wuk"], w["wuv"]]
    est = 2 * sum(int(a.size) * a.dtype.itemsize for a in weights) + 2 * tm * (4096 + 4 * 512 + 1024 + 512 + 3 * 2 * HP) + 6 * tm * HP * 4
    return pl.pallas_call(
        _mla_prep_kernel,
        grid=(T // tm,),
        in_specs=[row(D_MODEL)] + [full(a) for a in weights] + [row(LANES)] * 4,
        out_specs=[row(MLA_KV_LORA), row(LANES), row(HP), row(HP), row(HP)],
        out_shape=[jax.ShapeDtypeStruct((T, MLA_KV_LORA), F32), jax.ShapeDtypeStruct((T, LANES), F32),
                   jax.ShapeDtypeStruct((T, HP), BF16), jax.ShapeDtypeStruct((T, HP), BF16),
                   jax.ShapeDtypeStruct((T, HP), BF16)],
        compiler_params=_cparams(("parallel",), est),
        name="mla_prep",
    )(x, *weights, *tabs)


def _nsa_proj_kernel(x_ref, g_ref, wq_ref, wkv_ref, wkvp_ref, wg_ref,
                     q_ref, cmp_ref, sel_ref, win_ref, kvp_ref, gate_ref):
    xn = _rms(x_ref[...], g_ref[...]).astype(BF16)
    dot = functools.partial(jnp.dot, preferred_element_type=F32)
    q_ref[...] = dot(xn, wq_ref[...]).astype(BF16)
    kv = dot(xn, wkv_ref[...])
    cmp_ref[...] = kv[:, 0:256]
    sel_ref[...] = kv[:, 256:512]
    win_ref[...] = kv[:, 512:768]
    kvp_ref[...] = dot(xn, wkvp_ref[...]).astype(BF16)
    gate_ref[...] = jax.nn.sigmoid(dot(xn, wg_ref[...]))


def nsa_proj(x, g, w):
    T = x.shape[0]
    tm = _row_tile(T, 512)
    row = lambda n: pl.BlockSpec((tm, n), lambda i: (i, 0))
    full = lambda a: pl.BlockSpec(a.shape, lambda i: (0,) * a.ndim)
    weights = [g.reshape(1, -1), w["wq"], w["wkv"], w["wkvp"], w["wg"]]
    est = 2 * sum(int(a.size) * a.dtype.itemsize for a in weights) + 2 * tm * (4096 + 2 * 2048 + 4 * 768 + 2 * 1024 + 512) + 4 * tm * 4096
    return pl.pallas_call(
        _nsa_proj_kernel,
        grid=(T // tm,),
        in_specs=[row(D_MODEL)] + [full(a) for a in weights],
        out_specs=[row(NSA_HEADS * LANES), row(256), row(256), row(256), row(8 * LANES), row(LANES)],
        out_shape=[jax.ShapeDtypeStruct((T, NSA_HEADS * LANES), BF16)] + [jax.ShapeDtypeStruct((T, 256), F32)] * 3
                  + [jax.ShapeDtypeStruct((T, 8 * LANES), BF16), jax.ShapeDtypeStruct((T, LANES), F32)],
        compiler_params=_cparams(("parallel",), est),
        name="nsa_proj",
    )(x, *weights)


def _oproj_kernel(*refs, mode):
    if mode == "plain":
        o_ref, w_ref, g_ref, x_ref, out_ref = refs
        o = o_ref[...]
    elif mode == "dil":
        o0, o1, o2, l0, l1, l2, w_ref, g_ref, x_ref, out_ref = refs
        a0, a1, a2 = l0[...], l1[...], l2[...]
        m = jnp.maximum(jnp.maximum(a0, a1), a2)
        e0, e1, e2 = jnp.exp(a0 - m), jnp.exp(a1 - m), jnp.exp(a2 - m)
        o = (e0 * o0[...] + e1 * o1[...] + e2 * o2[...]) / (e0 + e1 + e2)
    else:
        oc, os_, ow, gt_ref, w_ref, g_ref, x_ref, out_ref = refs
        gt = gt_ref[...]
        parts = []
        for h in range(NSA_HEADS):
            sl = slice(h * LANES, (h + 1) * LANES)
            parts.append(gt[:, h:h + 1] * oc[:, sl].astype(F32)
                         + gt[:, NSA_HEADS + h:NSA_HEADS + h + 1] * os_[:, sl].astype(F32)
                         + gt[:, 2 * NSA_HEADS + h:2 * NSA_HEADS + h + 1] * ow[:, sl].astype(F32))
        o = jnp.concatenate(parts, axis=1)
    y = jnp.dot(o.astype(BF16), w_ref[...], preferred_element_type=F32)
    out_ref[...] = x_ref[...] + _rms(y, g_ref[...])


def oproj(mode, acts, w, g, x):
    T = x.shape[0]
    tm = _row_tile(T, 512)
    row = lambda a: pl.BlockSpec((tm, a.shape[1]), lambda i: (i, 0))
    full = lambda a: pl.BlockSpec(a.shape, lambda i: (0,) * a.ndim)
    g2 = g.reshape(1, -1)
    est = 2 * (sum(tm * a.shape[1] * a.dtype.itemsize for a in acts) + int(w.size) * 2 + 2 * tm * D_MODEL * 4) + 4 * tm * 2048 * 4
    return pl.pallas_call(
        functools.partial(_oproj_kernel, mode=mode),
        grid=(T // tm,),
        in_specs=[row(a) for a in acts] + [full(w), full(g2), row(x)],
        out_specs=row(x),
        out_shape=jax.ShapeDtypeStruct(x.shape, F32),
        compiler_params=_cparams(("parallel",), est),
        name="oproj_" + mode,
    )(*acts, w, g2, x)


def _mlp_kernel(x_ref, g2_ref, w1_ref, w2_ref, g3_ref, o_ref, xn_scr, acc_scr):
    f = pl.program_id(1)

    @pl.when(f == 0)
    def _():
        xn_scr[...] = _rms(x_ref[...], g2_ref[...]).astype(BF16)
        acc_scr[...] = jnp.zeros(acc_scr.shape, F32)

    h = jnp.dot(xn_scr[...], w1_ref[...], preferred_element_type=F32)
    h = jnp.square(jnp.maximum(h, 0.0)).astype(BF16)
    acc_scr[...] += jnp.dot(h, w2_ref[...], preferred_element_type=F32)

    @pl.when(f == pl.num_programs(1) - 1)
    def _():
        o_ref[...] = x_ref[...] + _rms(acc_scr[...], g3_ref[...])


def mlp(x, g2, w1, w2, g3):
    T = x.shape[0]
    tm = _row_tile(T, 512)
    tf = 1024
    est = 2 * (2 * tm * D_MODEL * 4 + 2 * D_MODEL * tf * 2) + tm * D_MODEL * 6 + 2 * tm * tf * 4
    return pl.pallas_call(
        _mlp_kernel,
        grid=(T // tm, D_FF // tf),
        in_specs=[pl.BlockSpec((tm, D_MODEL), lambda i, f: (i, 0)),
                  pl.BlockSpec((1, D_MODEL), lambda i, f: (0, 0)),
                  pl.BlockSpec((D_MODEL, tf), lambda i, f: (0, f)),
                  pl.BlockSpec((tf, D_MODEL), lambda i, f: (f, 0)),
                  pl.BlockSpec((1, D_MODEL), lambda i, f: (0, 0))],
        out_specs=pl.BlockSpec((tm, D_MODEL), lambda i, f: (i, 0)),
        out_shape=jax.ShapeDtypeStruct(x.shape, F32),
        scratch_shapes=[pltpu.VMEM((tm, D_MODEL), BF16), pltpu.VMEM((tm, D_MODEL), F32)],
        compiler_params=_cparams(("parallel", "arbitrary"), est),
        name="mlp",
    )(x, g2.reshape(1, -1), w1, w2, g3.reshape(1, -1))


def _head_mm_kernel(x_ref, w_ref, o_ref):
    o_ref[...] = jnp.dot(x_ref[...].astype(BF16), w_ref[...], preferred_element_type=F32).astype(o_ref.dtype)


def head_mm(x, w, out_dtype):
    M = x.shape[0]
    H, kb, nb = w.shape
    return pl.pallas_call(
        _head_mm_kernel,
        grid=(H,),
        in_specs=[pl.BlockSpec((M, kb), lambda h: (0, h)), pl.BlockSpec((None, kb, nb), lambda h: (h, 0, 0))],
        out_specs=pl.BlockSpec((M, nb), lambda h: (0, h)),
        out_shape=jax.ShapeDtypeStruct((M, H * nb), out_dtype),
        compiler_params=_cparams(("parallel",), 4 * (M * kb * 4 + kb * nb * 2 + M * nb * 4)),
        name="head_mm",
    )(x, w)


class FlashCfg(NamedTuple):
    R: int
    kv_shared: bool
    tq: int
    tk: int
    window: Optional[int]
    alibi: bool
    dist_scale: float
    slope_stride: int
    slope_base: int
    q_scale: float
    has_sel: bool
    want_lse: bool


def _pair_tables(nq, nk, tq, tk, window):
    qi, kj = [], []
    for i in range(nq):
        q_lo, q_hi = i * tq, i * tq + tq - 1
        for j in range(nk):
            k_lo, k_hi = j * tk, j * tk + tk - 1
            if k_lo <= q_hi and (window is None or k_hi >= q_lo - window):
                qi.append(i)
                kj.append(j)
    n = len(qi)
    first = [1 if (p == 0 or qi[p] != qi[p - 1]) else 0 for p in range(n)]
    last = [1 if (p == n - 1 or qi[p] != qi[p + 1]) else 0 for p in range(n)]
    return tuple(jnp.asarray(np.asarray(a, np.int32)) for a in (qi, kj, first, last))


def _flash_kernel(qi_ref, kj_ref, fi_ref, la_ref, sl_ref, *refs, cfg):
    R, tq, tk = cfg.R, cfg.tq, cfg.tk
    it = iter(refs)
    q_ref, k_ref, v_ref = next(it), next(it), next(it)
    sel_ref = next(it) if cfg.has_sel else None
    e_ref = next(it) if cfg.has_sel else None
    o_ref = next(it)
    lse_ref = next(it) if cfg.want_lse else None
    m_scr, l_scr, acc_scr = next(it), next(it), next(it)
    n2 = pl.program_id(1)
    p = pl.program_id(2)
    qi, kj = qi_ref[p], kj_ref[p]

    @pl.when(fi_ref[p] == 1)
    def _():
        m_scr[...] = jnp.full(m_scr.shape, NEG_INF, F32)
        l_scr[...] = jnp.zeros(l_scr.shape, F32)
        acc_scr[...] = jnp.zeros(acc_scr.shape, F32)

    dist = (qi * tq + lax.broadcasted_iota(I32, (tq, tk), 0)) - (kj * tk + lax.broadcasted_iota(I32, (tq, tk), 1))
    mask = dist >= 0
    if cfg.window is not None:
        mask = mask & (dist <= cfg.window)
    if cfg.has_sel:
        mask = mask & (jnp.dot(sel_ref[...], e_ref[...], preferred_element_type=F32) > 0.5)
    if cfg.alibi:
        distf = dist.astype(F32) * cfg.dist_scale
    for r in range(R):
        sl = slice(r * LANES, (r + 1) * LANES)
        ksl = slice(0, LANES) if cfg.kv_shared else sl
        q = q_ref[:, sl]
        if cfg.q_scale != 1.0 or q.dtype != BF16:
            q = (q.astype(F32) * cfg.q_scale).astype(BF16)
        s = lax.dot_general(q, k_ref[:, ksl].astype(BF16), (((1,), (1,)), ((), ())), preferred_element_type=F32)
        if cfg.alibi:
            s = s - sl_ref[cfg.slope_base + cfg.slope_stride * n2 + r] * distf
        s = jnp.where(mask, s, NEG_INF)
        m_prev = m_scr[r]
        m_new = jnp.maximum(m_prev, jnp.max(s, axis=1, keepdims=True))
        pr = jnp.where(mask, jnp.exp(s - m_new), 0.0)
        alpha = jnp.exp(m_prev - m_new)
        l_scr[r] = alpha * l_scr[r] + jnp.sum(pr, axis=1, keepdims=True)
        acc_scr[r] = alpha * acc_scr[r] + jnp.dot(pr.astype(BF16), v_ref[:, ksl].astype(BF16), preferred_element_type=F32)
        m_scr[r] = m_new

    @pl.when(la_ref[p] == 1)
    def _():
        for r in range(R):
            sl = slice(r * LANES, (r + 1) * LANES)
            l = jnp.maximum(l_scr[r], 1e-30)
            o_ref[:, sl] = (acc_scr[r] / l).astype(o_ref.dtype)
            if cfg.want_lse:
                lse_ref[:, sl] = jnp.broadcast_to(m_scr[r] + jnp.log(l), (tq, LANES))


def flash(cfg, q, k, v, slopes, n1, n2, q_col, k_col, v_col, o_col, o_shape, o_dtype, sel=None, emat=None):
    R, tq, tk = cfg.R, cfg.tq, cfg.tk
    Sq, Sk = q.shape[1], k.shape[1]
    tabs = _pair_tables(Sq // tq, Sk // tk, tq, tk, cfg.window)
    P = int(tabs[0].shape[0])
    kw = LANES if cfg.kv_shared else R * LANES
    in_specs = [pl.BlockSpec((None, tq, R * LANES), lambda a, b, p, qi, kj, fi, la, sl: (a, qi[p], q_col(b))),
                pl.BlockSpec((None, tk, kw), lambda a, b, p, qi, kj, fi, la, sl: (a, kj[p], k_col(b))),
                pl.BlockSpec((None, tk, kw), lambda a, b, p, qi, kj, fi, la, sl: (a, kj[p], v_col(b)))]
    args = [q, k, v]
    if cfg.has_sel:
        in_specs += [pl.BlockSpec((None, None, tq, LANES), lambda a, b, p, qi, kj, fi, la, sl: (a, b, qi[p], 0)),
                     pl.BlockSpec((LANES, tk), lambda a, b, p, qi, kj, fi, la, sl: (0, kj[p]))]
        args += [sel, emat]
    o_spec = pl.BlockSpec((None, tq, R * LANES), lambda a, b, p, qi, kj, fi, la, sl: (a, qi[p], o_col(b)))
    out_specs = [o_spec]
    out_shape = [jax.ShapeDtypeStruct(o_shape, o_dtype)]
    if cfg.want_lse:
        out_specs.append(o_spec)
        out_shape.append(jax.ShapeDtypeStruct(o_shape, F32))
    est = 2 * (tq * R * LANES * q.dtype.itemsize + 2 * tk * kw * k.dtype.itemsize + 2 * tq * R * LANES * 4) \
        + R * tq * LANES * 4 * 3 + 6 * tq * tk * 4
    outs = pl.pallas_call(
        functools.partial(_flash_kernel, cfg=cfg),
        grid_spec=pltpu.PrefetchScalarGridSpec(
            num_scalar_prefetch=5, grid=(n1, n2, P), in_specs=in_specs, out_specs=out_specs,
            scratch_shapes=[pltpu.VMEM((R, tq, 1), F32), pltpu.VMEM((R, tq, 1), F32), pltpu.VMEM((R, tq, LANES), F32)]),
        out_shape=out_shape,
        compiler_params=_cparams(("parallel", "parallel", "arbitrary"), est),
        name="flash",
    )(*tabs, slopes, *args)
    return outs if cfg.want_lse else outs[0]


PAGES_PER_STEP = 16


def _mla_decode_kernel(pt_ref, ql_ref, qr_ref, cn_ref, kn_ref, *refs):
    npg = PAGES_PER_STEP
    lat_refs, rope_refs = refs[:npg], refs[npg:2 * npg]
    o_ref, m_scr, l_scr, acc_scr = refs[2 * npg:]
    j = pl.program_id(1)

    @pl.when(j == 0)
    def _():
        m_scr[...] = jnp.full(m_scr.shape, NEG_INF, F32)
        l_scr[...] = jnp.zeros(l_scr.shape, F32)
        acc_scr[...] = jnp.zeros(acc_scr.shape, F32)

    ql, qr = ql_ref[...], qr_ref[...]
    kl = jnp.concatenate([r[...] for r in lat_refs], axis=0)
    kr = jnp.concatenate([r[...] for r in rope_refs], axis=0)
    nt = (((1,), (1,)), ((), ()))
    s = lax.dot_general(ql, kl, nt, preferred_element_type=F32) + lax.dot_general(qr, kr, nt, preferred_element_type=F32)
    m_prev = m_scr[...]
    m_new = jnp.maximum(m_prev, jnp.max(s, axis=1, keepdims=True))
    pr = jnp.exp(s - m_new)
    alpha = jnp.exp(m_prev - m_new)
    l_scr[...] = alpha * l_scr[...] + jnp.sum(pr, axis=1, keepdims=True)
    acc_scr[...] = alpha * acc_scr[...] + jnp.dot(pr, kl, preferred_element_type=F32)
    m_scr[...] = m_new

    @pl.when(j == pl.num_programs(1) - 1)
    def _():
        cn, kn = cn_ref[...], kn_ref[...]
        s_new = jnp.sum(ql * cn, axis=1, keepdims=True) + jnp.sum(qr * kn, axis=1, keepdims=True)
        m_prev = m_scr[...]
        m_fin = jnp.maximum(m_prev, s_new)
        alpha = jnp.exp(m_prev - m_fin)
        p_new = jnp.exp(s_new - m_fin)
        l = alpha * l_scr[...] + p_new
        o_ref[...] = (alpha * acc_scr[...] + p_new * cn) / l


def mla_decode(page_table, q_lat, q_rope, ckv_new, kr_new, cache_lat, cache_rope):
    DB, NP = page_table.shape
    npg = PAGES_PER_STEP
    assert NP % npg == 0
    H, C, Rr = MLA_HEADS, MLA_KV_LORA, MLA_ROPE
    seq = lambda shp: pl.BlockSpec((None,) + shp, lambda b, j, pt: (b, 0, 0))
    page = lambda i, w: pl.BlockSpec((None, PAGE, w), lambda b, j, pt: (pt[b, j * npg + i], 0, 0))
    in_specs = [seq((H, C)), seq((H, Rr)), seq((1, C)), seq((1, Rr))] \
        + [page(i, C) for i in range(npg)] + [page(i, Rr) for i in range(npg)]
    est = 2 * npg * PAGE * (C + LANES) * 4 + 3 * npg * PAGE * C * 4
    return pl.pallas_call(
        _mla_decode_kernel,
        grid_spec=pltpu.PrefetchScalarGridSpec(
            num_scalar_prefetch=1, grid=(DB, NP // npg), in_specs=in_specs,
            out_specs=pl.BlockSpec((None, H, C), lambda b, j, pt: (b, 0, 0)),
            scratch_shapes=[pltpu.VMEM((H, 1), F32), pltpu.VMEM((H, 1), F32), pltpu.VMEM((H, C), F32)]),
        out_shape=jax.ShapeDtypeStruct((DB, H, C), F32),
        compiler_params=_cparams(("parallel", "arbitrary"), est),
        name="mla_decode",
    )(page_table, q_lat, q_rope, ckv_new, kr_new, *([cache_lat] * npg), *([cache_rope] * npg))


DIL_ROWS = DIL_GROUPS[0][0] // DIL_GROUPS[0][1]


def _dil_decode_kernel(sl_ref, z_ref, c0_ref, c1_ref, c2_ref, o_ref, lse_ref):
    z = z_ref[...]
    nt = (((1,), (1,)), ((), ()))
    row = lax.broadcasted_iota(I32, (1, DIL_ROWS), 1)
    for g, (c_ref, (_, dil)) in enumerate(zip((c0_ref, c1_ref, c2_ref), DIL_GROUPS)):
        dist = ((DIL_ROWS - row) * dil).astype(F32)
        for h in range(DIL_HEADS):
            base = (g * 3 * DIL_HEADS + h) * LANES
            q = z[:, base:base + LANES] * (DIL_HD ** -0.5)
            kn = z[:, base + DIL_HEADS * LANES: base + (DIL_HEADS + 1) * LANES]
            vn = z[:, base + 2 * DIL_HEADS * LANES: base + (2 * DIL_HEADS + 1) * LANES]
            kc = c_ref[:, 0, h, :]
            vc = c_ref[:, 1, h, :]
            q8 = jnp.broadcast_to(q, (8, LANES)).astype(BF16)
            s = lax.dot_general(q8, kc.astype(BF16), nt, preferred_element_type=F32)[0:1] \
                - sl_ref[g * DIL_HEADS + h] * dist
            s_new = jnp.sum(q.astype(BF16).astype(F32) * kn.astype(BF16).astype(F32), axis=1, keepdims=True)
            m = jnp.maximum(jnp.max(s, axis=1, keepdims=True), s_new)
            pr = jnp.exp(s - m)
            p_new = jnp.exp(s_new - m)
            l = jnp.sum(pr, axis=1, keepdims=True) + p_new
            pv = jnp.dot(jnp.broadcast_to(pr, (8, DIL_ROWS)).astype(BF16), vc.astype(BF16), preferred_element_type=F32)[0:1]
            o_ref[g, :, h * LANES:(h + 1) * LANES] = (pv + p_new * vn) / l
            lse_ref[g, :, h * LANES:(h + 1) * LANES] = jnp.broadcast_to(m + jnp.log(l), (1, LANES))


def dil_decode(slopes, z_dec, caches):
    DB = z_dec.shape[0]
    NG = len(DIL_GROUPS)
    views = []
    for c, (w, d) in zip(caches, DIL_GROUPS):
        assert c.shape[1] == w and w // d == DIL_ROWS, "decode window buffers must be full"
        views.append(c.reshape(DB, w // d, d, 2, DIL_HEADS, DIL_HD))
    out_spec = pl.BlockSpec((NG, None, 1, DIL_HEADS * LANES), lambda b, sl: (0, b, 0, 0))
    shp = jax.ShapeDtypeStruct((NG, DB, 1, DIL_HEADS * LANES), F32)
    return pl.pallas_call(
        _dil_decode_kernel,
        grid_spec=pltpu.PrefetchScalarGridSpec(
            num_scalar_prefetch=1, grid=(DB,),
            in_specs=[pl.BlockSpec((None, 1, z_dec.shape[1]), lambda b, sl: (b, 0, 0))]
                     + [pl.BlockSpec((None, DIL_ROWS, None, 2, DIL_HEADS, DIL_HD), lambda b, sl: (b, 0, 0, 0, 0, 0))] * NG,
            out_specs=[out_spec, out_spec]),
        out_shape=[shp, shp],
        compiler_params=_cparams(("parallel",), 2 * NG * DIL_ROWS * 2 * 8 * LANES * 4 * 2),
        name="dil_decode",
    )(slopes, z_dec.reshape(DB, 1, -1), *views)


SHIFT_BLOCK_BYTES = 2 << 20


def _shift_kernel(c_ref, nxt_ref, new_ref, o_ref, *, n_chunks):
    wc = c_ref.shape[1]
    o_ref[:, 0:wc - 1] = c_ref[:, 1:wc]
    if n_chunks == 1:
        o_ref[:, wc - 1:wc] = new_ref[...]
    else:
        last = pl.program_id(1) == n_chunks - 1

        @pl.when(last)
        def _():
            o_ref[:, wc - 1:wc] = new_ref[...]

        @pl.when(jnp.logical_not(last))
        def _():
            o_ref[:, wc - 1:wc] = nxt_ref[...]


def shift_append(cache, new):
    DB, W = cache.shape[:2]
    rest = cache.shape[2:]
    row_bytes = int(np.prod(rest)) * cache.dtype.itemsize
    wc = W
    while wc * row_bytes > SHIFT_BLOCK_BYTES and wc % 2 == 0:
        wc //= 2
    nb = 1
    while nb * 2 * wc * row_bytes <= SHIFT_BLOCK_BYTES and DB % (nb * 2) == 0:
        nb *= 2
    n_chunks = W // wc
    zeros = (0,) * len(rest)
    blk = lambda rows: (nb, rows) + rest
    return pl.pallas_call(
        functools.partial(_shift_kernel, n_chunks=n_chunks),
        grid=(DB // nb, n_chunks),
        in_specs=[pl.BlockSpec(blk(wc), lambda b, c: (b, c) + zeros),
                  pl.BlockSpec(blk(1), lambda b, c: (b, jnp.minimum((c + 1) * wc, W - 1)) + zeros),
                  pl.BlockSpec(blk(1), lambda b, c: (b, 0) + zeros)],
        out_specs=pl.BlockSpec(blk(wc), lambda b, c: (b, c) + zeros),
        out_shape=jax.ShapeDtypeStruct(cache.shape, cache.dtype),
        compiler_params=_cparams(("parallel", "parallel"), 8 * nb * wc * row_bytes),
        name="shift_append",
    )(cache, cache, new)


SHIFT_CHUNKS = 8


def _shift_dma_kernel(c_ref, n_ref, o_ref, sem):
    DB, W = c_ref.shape[0], c_ref.shape[1]
    nb = DB // SHIFT_CHUNKS if DB % SHIFT_CHUNKS == 0 else DB
    copies = []
    for i, b0 in enumerate(range(0, DB, nb)):
        copies.append(pltpu.make_async_copy(c_ref.at[pl.ds(b0, nb), pl.ds(1, W - 1)],
                                            o_ref.at[pl.ds(b0, nb), pl.ds(0, W - 1)], sem.at[i]))
    copies.append(pltpu.make_async_copy(n_ref, o_ref.at[:, pl.ds(W - 1, 1)], sem.at[len(copies)]))
    for c in copies:
        c.start()
    for c in copies:
        c.wait()


def shift_append_dma(cache, new):
    return pl.pallas_call(
        _shift_dma_kernel,
        in_specs=[pl.BlockSpec(memory_space=pl.ANY), pl.BlockSpec(memory_space=pl.ANY)],
        out_specs=pl.BlockSpec(memory_space=pl.ANY),
        out_shape=jax.ShapeDtypeStruct(cache.shape, cache.dtype),
        scratch_shapes=[pltpu.SemaphoreType.DMA((SHIFT_CHUNKS + 1,))],
        name="shift_append_dma",
    )(cache, new)


NSA_ROW = 2 * NSA_KV_HEADS * NSA_HD
CHUNK_W = CMP_STRIDE * NSA_ROW


def _compress_kernel(*refs, n_blk, blk_rows, has_new, n_out):
    if has_new:
        pt_ref, refs = refs[0], refs[1:]
    a_refs = refs[:n_blk]
    pos = n_blk
    new_ref = None
    if has_new:
        new_ref = refs[pos]
        pos += 1
    pe_ref, w_ref, o_ref, p_scr = refs[pos:pos + 4]
    step = pl.program_id(1)
    rows = n_blk * blk_rows

    def parts(a, row0, nrows):
        for k in range(2):
            ak = jnp.concatenate([a[:, l * NSA_ROW + k * LANES: l * NSA_ROW + (k + 1) * LANES]
                                  for l in range(CMP_STRIDE)], axis=1)
            for half in range(2):
                lhs = (ak + pe_ref[k, half]).astype(BF16)
                p_scr[k, half, pl.ds(row0, nrows), :] = jnp.dot(lhs, w_ref[k, half], preferred_element_type=F32)

    a = a_refs[0][...] if n_blk == 1 else jnp.concatenate([r[...] for r in a_refs], axis=0)
    parts(a, pl.multiple_of(step * rows, 8), rows)

    @pl.when(step == pl.num_programs(1) - 1)
    def _():
        if has_new:
            new_chunk = jnp.concatenate([new_ref[...], jnp.zeros((1, CHUNK_W - NSA_ROW), F32)], axis=1)
            parts(jnp.concatenate([new_chunk, jnp.zeros((7, CHUNK_W), F32)], axis=0), n_out, 8)
        for k in range(2):
            o_ref[k] = jnp.zeros(o_ref.shape[1:], o_ref.dtype)
            o_ref[k, 0:n_out, :] = (p_scr[k, 0, 0:n_out, :] + p_scr[k, 1, 1:n_out + 1, :]).astype(o_ref.dtype)


def nsa_compress_prompt(kv_cmp, pe, w):
    B, S, _ = kv_cmp.shape
    n_ch = S // CMP_STRIDE
    a = kv_cmp.reshape(B, n_ch, CHUNK_W)
    kern = functools.partial(_compress_kernel, n_blk=1, blk_rows=n_ch, has_new=False, n_out=n_ch - 1)
    return pl.pallas_call(
        kern,
        grid=(B, 1),
        in_specs=[pl.BlockSpec((None, n_ch, CHUNK_W), lambda b, s: (b, 0, 0)),
                  pl.BlockSpec(pe.shape, lambda b, s: (0, 0, 0, 0)),
                  pl.BlockSpec(w.shape, lambda b, s: (0, 0, 0, 0))],
        out_specs=pl.BlockSpec((None, 2, n_ch, 2 * LANES), lambda b, s: (b, 0, 0, 0)),
        out_shape=jax.ShapeDtypeStruct((B, 2, n_ch, 2 * LANES), BF16),
        scratch_shapes=[pltpu.VMEM((2, 2, n_ch, 2 * LANES), F32)],
        compiler_params=_cparams(("parallel", "arbitrary"), 3 * n_ch * CHUNK_W * 4 + 2 * int(w.size) * 2),
        name="nsa_compress_prompt",
    )(a, pe, w)


def nsa_compress_decode(page_table, pool, new_row, pe, w):
    DB, NP = page_table.shape
    npg = PAGES_PER_STEP
    assert NP % npg == 0
    cpp = PAGE // CMP_STRIDE
    n_ch = NP * cpp
    pool_v = pool.reshape(pool.shape[0], cpp, CHUNK_W)
    kern = functools.partial(_compress_kernel, n_blk=npg, blk_rows=cpp, has_new=True, n_out=n_ch)
    page = lambda i: pl.BlockSpec((None, cpp, CHUNK_W), lambda b, s, pt: (pt[b, s * npg + i], 0, 0))
    return pl.pallas_call(
        kern,
        grid_spec=pltpu.PrefetchScalarGridSpec(
            num_scalar_prefetch=1, grid=(DB, NP // npg),
            in_specs=[page(i) for i in range(npg)]
                     + [pl.BlockSpec((None, 1, NSA_ROW), lambda b, s, pt: (b, 0, 0)),
                        pl.BlockSpec(pe.shape, lambda b, s, pt: (0, 0, 0, 0)),
                        pl.BlockSpec(w.shape, lambda b, s, pt: (0, 0, 0, 0))],
            out_specs=pl.BlockSpec((None, 2, n_ch, 2 * LANES), lambda b, s, pt: (b, 0, 0, 0)),
            scratch_shapes=[pltpu.VMEM((2, 2, n_ch + 8, 2 * LANES), F32)]),
        out_shape=jax.ShapeDtypeStruct((DB, 2, n_ch, 2 * LANES), BF16),
        compiler_params=_cparams(("parallel", "arbitrary"),
                                 4 * npg * cpp * CHUNK_W * 4 + 2 * int(w.size) * 2 + 8 * n_ch * 256 * 4),
        name="nsa_compress_decode",
    )(page_table, *([pool_v] * npg), new_row, pe, w)


def _cmp_attn_kernel(sl_ref, q_ref, ck_ref, cv_ref, ov_ref, oc_ref, sel_ref, idx_ref, *, tq, n_cmp, n_sel, q_off, stacked):
    g = pl.program_id(1)
    qi = pl.program_id(2)
    G = NSA_GROUP
    if stacked:
        qs = q_ref[...]
    else:
        qs = jnp.concatenate([q_ref[:, r * LANES:(r + 1) * LANES] for r in range(G)], axis=0)
    nc = ck_ref.shape[0]
    rows = G * tq
    s = lax.dot_general(qs, ck_ref[...], (((1,), (1,)), ((), ())), preferred_element_type=F32)
    ridx = lax.broadcasted_iota(I32, (rows, 1), 0)
    qpos_r = q_off + qi * tq + (ridx & (tq - 1))
    slope = jnp.zeros((rows, 1), F32)
    for r in range(G):
        slope = jnp.where((ridx >= r * tq) & (ridx < (r + 1) * tq), sl_ref[g * G + r], slope)
    n_iota = lax.broadcasted_iota(I32, (1, nc), 1)
    dist = qpos_r - (n_iota * CMP_STRIDE + CMP_LEN - 1)
    valid = (dist >= 0) & (n_iota < n_cmp)
    s = jnp.where(valid, s - slope * dist.astype(F32), NEG_INF)
    m = jnp.max(s, axis=1, keepdims=True)
    e = jnp.where(valid, jnp.exp(s - m), 0.0)
    pc = e / jnp.maximum(jnp.sum(e, axis=1, keepdims=True), 1e-30)
    oc = jnp.dot(pc.astype(BF16), cv_ref[...], preferred_element_type=F32)
    if stacked:
        oc_ref[...] = oc.astype(oc_ref.dtype)
        psum = jnp.sum(pc, axis=0, keepdims=True)
    else:
        for r in range(G):
            oc_ref[:, r * LANES:(r + 1) * LANES] = oc[r * tq:(r + 1) * tq].astype(oc_ref.dtype)
        psum = pc[0:tq]
        for r in range(1, G):
            psum = psum + pc[r * tq:(r + 1) * tq]
    hi = psum.astype(BF16)
    lo = (psum - hi.astype(F32)).astype(BF16)
    ov = ov_ref[...]
    imp = jnp.dot(hi, ov, preferred_element_type=F32) + jnp.dot(lo, ov, preferred_element_type=F32)
    ns = imp.shape[1]
    blk = lax.broadcasted_iota(I32, (tq, ns), 1)
    qpos = q_off + qi * tq + lax.broadcasted_iota(I32, (tq, ns), 0)
    cur = lax.shift_right_arithmetic(qpos, int(math.log2(SEL_LEN)))
    forced = (blk == 0) | (blk == cur) | (blk == cur - 1)
    score = jnp.where(forced, imp + FORCE_BONUS, imp)
    score = jnp.where(blk * SEL_LEN <= qpos, score, -1.0)
    score = jnp.where(blk < n_sel, score, -2.0)
    blkf = blk.astype(F32)
    chosen = jnp.zeros((tq, ns), F32)
    idx = jnp.zeros((tq, LANES), F32)
    lane = lax.broadcasted_iota(I32, (tq, LANES), 1)
    for t in range(SEL_TOP):
        mx = jnp.max(score, axis=1, keepdims=True)
        first = jnp.min(jnp.where(score == mx, blkf, float(ns)), axis=1, keepdims=True)
        hit = blkf == first
        chosen = jnp.where(hit, 1.0, chosen)
        score = jnp.where(hit, -3.0, score)
        idx = jnp.where(lane == t, first, idx)
    sel_ref[...] = chosen.astype(sel_ref.dtype)
    idx_ref[...] = idx.astype(I32)


def nsa_cmp_attn(slopes, q, ckv, ov, *, tq, n_cmp, n_sel, q_off, stacked):
    NB = q.shape[0]
    nc = ckv.shape[2]
    ns = ov.shape[1]
    G = NSA_GROUP
    if stacked:
        nq = 1
        q_spec = pl.BlockSpec((None, None, G, LANES), lambda b, g, i, sl: (b, g, 0, 0))
        oc_spec, oc_shape = q_spec, (NB, NSA_KV_HEADS, G, LANES)
    else:
        nq = q.shape[1] // tq
        q_spec = pl.BlockSpec((None, tq, G * LANES), lambda b, g, i, sl: (b, i, g))
        oc_spec, oc_shape = q_spec, q.shape
    sq = nq * tq
    kern = functools.partial(_cmp_attn_kernel, tq=tq, n_cmp=n_cmp, n_sel=n_sel, q_off=q_off, stacked=stacked)
    return pl.pallas_call(
        kern,
        grid_spec=pltpu.PrefetchScalarGridSpec(
            num_scalar_prefetch=1, grid=(NB, NSA_KV_HEADS, nq),
            in_specs=[q_spec,
                      pl.BlockSpec((None, None, nc, LANES), lambda b, g, i, sl: (b, 0, 0, g)),
                      pl.BlockSpec((None, None, nc, LANES), lambda b, g, i, sl: (b, 1, 0, g)),
                      pl.BlockSpec(ov.shape, lambda b, g, i, sl: (0, 0))],
            out_specs=[oc_spec,
                       pl.BlockSpec((None, None, tq, ns), lambda b, g, i, sl: (b, g, i, 0)),
                       pl.BlockSpec((None, None, tq, LANES), lambda b, g, i, sl: (b, g, i, 0))]),
        out_shape=[jax.ShapeDtypeStruct(oc_shape, BF16),
                   jax.ShapeDtypeStruct((NB, NSA_KV_HEADS, sq, ns), BF16),
                   jax.ShapeDtypeStruct((NB, NSA_KV_HEADS, sq, LANES), I32)],
        compiler_params=_cparams(("parallel", "parallel", "parallel"), 16 * G * tq * nc * 4),
        name="nsa_cmp_attn",
    )(slopes, q, ckv, ckv, ov)


N_SEL_SLOTS = NSA_KV_HEADS * SEL_TOP


def _nsa_decode_kernel(pt_ref, ix_ref, sl_ref, q_ref, sn_ref, wn_ref, win_ref, *refs, past):
    blk_refs = refs[:N_SEL_SLOTS]
    os_ref, ow_ref = refs[N_SEL_SLOTS:]
    b = pl.program_id(0)
    G = NSA_GROUP
    nt = (((1,), (1,)), ((), ()))
    ridx = lax.broadcasted_iota(I32, (G, 1), 0)
    lane = lax.broadcasted_iota(I32, (G, LANES), 1)
    n_last = past // SEL_LEN
    for g in range(NSA_KV_HEADS):
        q = q_ref[g].astype(F32)
        qk = q if g == 0 else pltpu.roll(q, NSA_HD, 1)
        qk_b = qk.astype(BF16)
        slope = jnp.zeros((G, 1), F32)
        for r in range(G):
            slope = jnp.where(ridx == r, sl_ref[g * G + r], slope)

        def branch(kv, dist, valid, new_row):
            s = lax.dot_general(qk_b, kv[:, 0:LANES].astype(BF16), nt, preferred_element_type=F32)
            s = jnp.where(valid, s - slope * dist, NEG_INF)
            kn = new_row[:, 0:LANES].astype(BF16).astype(F32)
            s_new = jnp.sum(qk_b.astype(F32) * kn, axis=1, keepdims=True)
            m = jnp.maximum(jnp.max(s, axis=1, keepdims=True), s_new)
            pr = jnp.where(valid, jnp.exp(s - m), 0.0)
            p_new = jnp.exp(s_new - m)
            l = jnp.sum(pr, axis=1, keepdims=True) + p_new
            o = (jnp.dot(pr.astype(BF16), kv[:, LANES:2 * LANES].astype(BF16), preferred_element_type=F32)
                 + p_new * new_row[:, LANES:2 * LANES]) / l
            o = o if g == 0 else pltpu.roll(o, NSA_HD, 1)
            return jnp.where(lane < NSA_HD, o, 0.0)

        kv = jnp.concatenate([blk_refs[g * SEL_TOP + j][...] for j in range(SEL_TOP)], axis=0)
        nk = SEL_TOP * SEL_LEN
        kidx = lax.broadcasted_iota(I32, (1, nk), 1)
        slot = lax.shift_right_arithmetic(kidx, int(math.log2(SEL_LEN)))
        kpos = jnp.zeros((1, nk), I32)
        ok = jnp.zeros((1, nk), jnp.bool_)
        for j in range(SEL_TOP):
            bj = ix_ref[b, g * SEL_TOP + j]
            kpos = jnp.where(slot == j, bj * SEL_LEN + (kidx - j * SEL_LEN), kpos)
            ok = ok | ((slot == j) & (bj < n_last))
        os_ref[g] = branch(kv, (past - kpos).astype(F32), ok, sn_ref[...]).astype(os_ref.dtype)
        wrows = win_ref.shape[0]
        wd = (wrows - lax.broadcasted_iota(I32, (1, wrows), 1)).astype(F32)
        ow_ref[g] = branch(win_ref[...], wd, wd > 0.0, wn_ref[...]).astype(ow_ref.dtype)


def nsa_decode(page_table, sel_idx, slopes, q, sel_new, win_new, cache_win, pool_sel, past):
    DB, NP = page_table.shape
    hp = PAGE // SEL_LEN
    pool_v = pool_sel.reshape(pool_sel.shape[0], hp, SEL_LEN, NSA_ROW)
    last_cached = NP * hp - 1

    def blk_spec(slot):
        def imap(b, pt, ix, sl):
            bj = jnp.minimum(ix[b, slot], last_cached)
            return (pt[b, bj // hp], bj % hp, 0, 0)
        return pl.BlockSpec((None, None, SEL_LEN, NSA_ROW), imap)

    q_spec = pl.BlockSpec((None, NSA_KV_HEADS, NSA_GROUP, LANES), lambda b, pt, ix, sl: (b, 0, 0, 0))
    row_spec = pl.BlockSpec((None, 1, NSA_ROW), lambda b, pt, ix, sl: (b, 0, 0))
    wrows = cache_win.shape[1]
    shp = jax.ShapeDtypeStruct((DB, NSA_KV_HEADS, NSA_GROUP, LANES), BF16)
    return pl.pallas_call(
        functools.partial(_nsa_decode_kernel, past=past),
        grid_spec=pltpu.PrefetchScalarGridSpec(
            num_scalar_prefetch=3, grid=(DB,),
            in_specs=[q_spec, row_spec, row_spec,
                      pl.BlockSpec((None, wrows, NSA_ROW), lambda b, pt, ix, sl: (b, 0, 0))]
                     + [blk_spec(s) for s in range(N_SEL_SLOTS)],
            out_specs=[q_spec, q_spec]),
        out_shape=[shp, shp],
        compiler_params=_cparams(("parallel",), 4 * (wrows + N_SEL_SLOTS * SEL_LEN) * NSA_ROW * 4),
        name="nsa_decode",
    )(page_table, sel_idx, slopes, q, sel_new, win_new, cache_win, *([pool_v] * N_SEL_SLOTS))


def _alibi_slopes(n):
    return 2.0 ** (-8.0 * jnp.arange(1, n + 1, dtype=F32) / n)


def _pad_last(a, n):
    return jnp.concatenate([a, jnp.zeros(a.shape[:-1] + (n - a.shape[-1],), a.dtype)], axis=-1)


def _rope_tables(pos, q_scale):
    half = MLA_ROPE // 2
    inv = ROPE_THETA ** (-jnp.arange(half, dtype=F32) / half)
    ang = pos.astype(F32)[:, None] * inv[None, :]
    cos, sin = jnp.cos(ang), jnp.sin(ang)
    n = pos.shape[0]
    one, z32 = jnp.ones((n, MLA_NOPE), F32), jnp.zeros((n, LANES - MLA_NOPE - MLA_ROPE), F32)
    c = jnp.concatenate([one, cos, cos, z32], axis=1)
    s = jnp.concatenate([0.0 * one, -sin, sin, z32], axis=1)
    return (q_scale * c, q_scale * s, c, s)


def _prep_mla(w_in, q_norm, w_qb, kv_norm, w_kvb, w_o):
    ql, kl, rp, half = MLA_Q_LORA, MLA_KV_LORA, MLA_ROPE, MLA_ROPE // 2
    wr = w_in[:, ql + kl:]
    z64 = jnp.zeros((D_MODEL, MLA_NOPE), F32)
    z32 = jnp.zeros((D_MODEL, LANES - MLA_NOPE - rp), F32)
    qb = w_qb.reshape(ql, MLA_HEADS, MLA_NOPE + rp)
    nope, rope = qb[..., :MLA_NOPE], qb[..., MLA_NOPE:]
    zq = jnp.zeros((ql, MLA_HEADS, MLA_NOPE), F32)
    uk, uv = w_kvb[..., :MLA_NOPE], w_kvb[..., MLA_NOPE:]
    bf = lambda a: a.astype(BF16)
    return dict(
        wq=bf(w_in[:, :ql]), wkv=bf(w_in[:, ql:ql + kl]),
        wr=bf(jnp.concatenate([z64, wr, z32], axis=1)),
        wrs=bf(jnp.concatenate([z64, wr[:, half:], wr[:, :half], z32], axis=1)),
        qn=q_norm.reshape(1, -1), kvn=kv_norm.reshape(1, -1),
        wqb=bf(_pad_last(jnp.concatenate([nope, rope], axis=-1), LANES).reshape(ql, -1)),
        wqbs=bf(_pad_last(jnp.concatenate([zq, rope[..., half:], rope[..., :half]], axis=-1), LANES).reshape(ql, -1)),
        wuk=bf(_pad_last(uk, LANES).reshape(kl, -1)),
        wuv=bf(_pad_last(uv, LANES).reshape(kl, -1)),
        wukT=bf(jnp.concatenate([jnp.transpose(uk, (1, 2, 0)), jnp.zeros((MLA_HEADS, LANES - MLA_NOPE, kl), F32)], axis=1)),
        wuvH=bf(_pad_last(jnp.transpose(uv, (1, 0, 2)), LANES)),
        wo=bf(_pad_last(w_o.reshape(MLA_HEADS, MLA_V, D_MODEL).transpose(0, 2, 1), LANES).transpose(0, 2, 1).reshape(-1, D_MODEL)),
    )


def _prep_nsa(w_in, cmp_w, cmp_pe, w_o):
    H, D, G = NSA_HEADS, NSA_HD, NSA_KV_HEADS
    qw = H * D
    kvw = 3 * 2 * G * D
    bf = lambda a: a.astype(BF16)
    wq = _pad_last((w_in[:, :qw] * (D ** -0.5)).reshape(D_MODEL, H, D), LANES).reshape(D_MODEL, -1)
    wkv = w_in[:, qw:qw + kvw]
    wkvp = _pad_last(wkv[:, 2 * G * D:].reshape(D_MODEL, 2 * 2 * G, D), LANES).reshape(D_MODEL, -1)
    wg = _pad_last(w_in[:, qw + kvw:], LANES)
    r = CMP_LEN // CMP_STRIDE
    base = cmp_w.reshape(2, r, CMP_STRIDE, D, D)
    wc = jnp.zeros((2, r, CMP_STRIDE, G, D, G, LANES), F32)
    for g in range(G):
        wc = wc.at[:, :, :, g, :, g, :D].set(base)
    pe = jnp.broadcast_to(cmp_pe.reshape(2, r, CMP_STRIDE, 1, D), (2, r, CMP_STRIDE, G, D))
    return dict(
        wq=bf(wq), wkv=bf(wkv), wkvp=bf(wkvp), wg=bf(wg),
        wc=bf(wc.reshape(2, r, CMP_STRIDE * G * D, G * LANES)),
        pe=pe.reshape(2, r, 1, CMP_STRIDE * G * D),
        wo=bf(_pad_last(w_o.reshape(H, D, D_MODEL).transpose(0, 2, 1), LANES).transpose(0, 2, 1).reshape(-1, D_MODEL)),
    )


def _overlap(nc, n_cmp, ns, n_sel):
    c0 = np.arange(nc)[:, None] * CMP_STRIDE
    s0 = np.arange(ns)[None, :] * SEL_LEN
    ov = np.maximum(np.minimum(c0 + CMP_LEN, s0 + SEL_LEN) - np.maximum(c0, s0), 0) / CMP_STRIDE
    ov = ov * (np.arange(nc)[:, None] < n_cmp) * (np.arange(ns)[None, :] < n_sel)
    return jnp.asarray(ov.astype(np.float32)).astype(BF16)


def _mla_layer(xp, xs, g0, g1, w, tabs_p, tabs_s, cache_lat, cache_rope, page_table, B, S):
    DB = xs.shape[0]
    HP = MLA_HEADS * LANES
    ckv_p, kr_p, q_p, k_p, v_p = mla_prep(xp, g0, w, tabs_p)
    ckv_s, kr_s, q_s, _, _ = mla_prep(xs, g0, w, tabs_s)
    R = 2
    t = _row_tile(S, 512)
    cfg = FlashCfg(R=R, kv_shared=False, tq=t, tk=t, window=None, alibi=False, dist_scale=1.0,
                   slope_stride=0, slope_base=0, q_scale=1.0, has_sel=False, want_lse=False)
    col = lambda h: h
    o_p = flash(cfg, q_p.reshape(B, S, HP), k_p.reshape(B, S, HP), v_p.reshape(B, S, HP), jnp.zeros((1,), F32),
                B, MLA_HEADS // R, col, col, col, col, (B, S, HP), BF16)
    q_lat = head_mm(q_s, w["wukT"], BF16).reshape(DB, MLA_HEADS, MLA_KV_LORA)
    q_rope = q_s.reshape(DB, MLA_HEADS, LANES)[:, :, MLA_NOPE:MLA_NOPE + MLA_ROPE]
    kr_s32 = kr_s[:, MLA_NOPE:MLA_NOPE + MLA_ROPE]
    o_lat = mla_decode(page_table, q_lat, q_rope, ckv_s.reshape(DB, 1, -1), kr_s32.reshape(DB, 1, -1), cache_lat, cache_rope)
    o_s = head_mm(o_lat.reshape(DB, -1), w["wuvH"], BF16)
    xp = oproj("plain", [o_p.reshape(B * S, HP)], w["wo"], g1, xp)
    xs = oproj("plain", [o_s], w["wo"], g1, xs)
    state = (ckv_p.reshape(B, S, -1), kr_p[:, MLA_NOPE:MLA_NOPE + MLA_ROPE].reshape(B, S, -1),
             ckv_s.reshape(DB, 1, -1), kr_s32.reshape(DB, 1, -1))
    return xp, xs, state


def _dil_layer(xp, xs, g0, g1, w_in, w_o, caches, B, S):
    DB = xs.shape[0]
    NG = len(DIL_GROUPS)
    zw = NG * 3 * DIL_HEADS * DIL_HD
    hw = DIL_HEADS * DIL_HD
    slopes = _alibi_slopes(NG * DIL_HEADS)
    w_in_b = w_in.astype(BF16)
    z_p = rms_proj(xp, g0, w_in_b, zw // 2)
    z_s = rms_proj(xs, g0, w_in_b, zw // 2)
    acts_o, acts_l = [], []
    for gi, (win, d) in enumerate(DIL_GROUPS):
        L = S // d
        assert win // d == LANES and L % LANES == 0
        zv = z_p.reshape(B, L, d * zw)
        cfg = FlashCfg(R=DIL_HEADS, kv_shared=False, tq=LANES, tk=LANES, window=win // d, alibi=True, dist_scale=float(d),
                       slope_stride=0, slope_base=gi * DIL_HEADS, q_scale=DIL_HD ** -0.5, has_sel=False, want_lse=True)
        upb = zw // hw
        o, lse = flash(cfg, zv, zv, zv, slopes, B, d,
                       lambda r, gi=gi: r * upb + gi * 3, lambda r, gi=gi: r * upb + gi * 3 + 1,
                       lambda r, gi=gi: r * upb + gi * 3 + 2, lambda r: r, (B, L, d * hw), F32)
        acts_o.append(o.reshape(B * S, hw))
        acts_l.append(lse.reshape(B * S, hw))
    o_d, lse_d = dil_decode(slopes, z_s, caches)
    w_o_b = w_o.astype(BF16)
    xp = oproj("dil", acts_o + acts_l, w_o_b, g1, xp)
    xs = oproj("dil", [o_d[i].reshape(DB, hw) for i in range(NG)] + [lse_d[i].reshape(DB, hw) for i in range(NG)], w_o_b, g1, xs)
    z3 = z_p.reshape(B, S, zw)
    st_p = [z3[:, S - min(win, S):, (3 * gi + 1) * hw:(3 * gi + 3) * hw].reshape(B, min(win, S), 2, DIL_HEADS, DIL_HD)
            for gi, (win, _) in enumerate(DIL_GROUPS)]
    zs6 = z_s.reshape(DB, NG, 3, DIL_HEADS * DIL_HD)
    st_s = []
    for gi, c in enumerate(caches):
        new = zs6[:, gi, 1:].reshape(DB, 1, 2, DIL_HEADS, DIL_HD)
        st_s.append(shift_append(c, new))
    return xp, xs, st_p, st_s


def _nsa_layer(xp, xs, g0, g1, w, cache_cmp, cache_sel, cache_win, page_table, B, S):
    DB, NP = page_table.shape
    past = NP * PAGE
    HP = NSA_HEADS * LANES
    slopes = _alibi_slopes(NSA_HEADS)
    q_p, cmp_p, sel_p, win_p, kvp_p, gt_p = nsa_proj(xp, g0, w)
    q_s, cmp_s, sel_s, win_s, _, gt_s = nsa_proj(xs, g0, w)
    n_ch = S // CMP_STRIDE
    ckv_p = nsa_compress_prompt(cmp_p.reshape(B, S, NSA_ROW), w["pe"], w["wc"])
    q3 = q_p.reshape(B, S, HP)
    oc_p, selmask, _ = nsa_cmp_attn(slopes, q3, ckv_p, _overlap(n_ch, n_ch - 1, LANES, S // SEL_LEN),
                                    tq=LANES, n_cmp=n_ch - 1, n_sel=S // SEL_LEN, q_off=0, stacked=False)
    kv3 = kvp_p.reshape(B, S, 8 * LANES)
    emat = jnp.asarray((np.arange(S)[None, :] // SEL_LEN == np.arange(LANES)[:, None]).astype(np.float32)).astype(BF16)
    base = dict(R=NSA_GROUP, kv_shared=True, alibi=True, dist_scale=1.0, slope_stride=NSA_GROUP, slope_base=0,
                q_scale=1.0, want_lse=False)
    cfg_s = FlashCfg(tq=256, tk=512, window=None, has_sel=True, **base)
    os_p = flash(cfg_s, q3, kv3, kv3, slopes, B, NSA_KV_HEADS, lambda g: g, lambda g: g, lambda g: 2 + g, lambda g: g,
                 (B, S, HP), BF16, sel=selmask, emat=emat)
    cfg_w = FlashCfg(tq=256, tk=256, window=NSA_WIN, has_sel=False, **base)
    ow_p = flash(cfg_w, q3, kv3, kv3, slopes, B, NSA_KV_HEADS, lambda g: g, lambda g: 4 + g, lambda g: 6 + g, lambda g: g,
                 (B, S, HP), BF16)
    xp = oproj("nsa", [oc_p.reshape(B * S, HP), os_p.reshape(B * S, HP), ow_p.reshape(B * S, HP), gt_p], w["wo"], g1, xp)
    pool_cmp = cache_cmp.reshape(cache_cmp.shape[0], PAGE, NSA_ROW)
    pool_sel = cache_sel.reshape(cache_sel.shape[0], PAGE, NSA_ROW)
    cwin = cache_win.reshape(DB, cache_win.shape[1], NSA_ROW)
    assert cwin.shape[1] == NSA_WIN
    ckv_s = nsa_compress_decode(page_table, pool_cmp, cmp_s.reshape(DB, 1, NSA_ROW), w["pe"], w["wc"])
    nc_s = NP * (PAGE // CMP_STRIDE)
    n_sel_s = past // SEL_LEN + 1
    q4 = q_s.reshape(DB, NSA_KV_HEADS, NSA_GROUP, LANES)
    oc_s, _, idx = nsa_cmp_attn(slopes, q4, ckv_s, _overlap(nc_s, nc_s, 2 * LANES, n_sel_s),
                                tq=1, n_cmp=nc_s, n_sel=n_sel_s, q_off=past, stacked=True)
    sel_idx = idx[:, :, 0, :SEL_TOP].reshape(DB, N_SEL_SLOTS)
    os_s, ow_s = nsa_decode(page_table, sel_idx, slopes, q4, sel_s.reshape(DB, 1, NSA_ROW), win_s.reshape(DB, 1, NSA_ROW),
                            cwin, pool_sel, past)
    xs = oproj("nsa", [oc_s.reshape(DB, HP), os_s.reshape(DB, HP), ow_s.reshape(DB, HP), gt_s], w["wo"], g1, xs)
    kvshape = (2, NSA_KV_HEADS, NSA_HD)
    st_p = (cmp_p.reshape(B, S, *kvshape), sel_p.reshape(B, S, *kvshape),
            win_p.reshape(B, S, *kvshape)[:, S - min(NSA_WIN, S):])
    win_new = shift_append_dma(cache_win, win_s.reshape(DB, 1, *kvshape))
    st_s = (cmp_s.reshape(DB, 1, *kvshape), sel_s.reshape(DB, 1, *kvshape), win_new)
    return xp, xs, st_p, st_s


def kernel(x_prompt, x_sample, cache_mla_latent, cache_mla_rope, cache_dil_kv0, cache_dil_kv1, cache_dil_kv2,
           cache_nsa_cmp, cache_nsa_sel, cache_nsa_win, page_table, norm_gains, mlp_w1, mlp_w2, mla_w_in,
           mla_q_norm, mla_w_qb, mla_kv_norm, mla_w_kvb, mla_w_o, dil_w_in, dil_w_o, nsa_w_in, nsa_cmp_w,
           nsa_cmp_pe, nsa_w_o):
    B, S, _ = x_prompt.shape
    DB, n_s, _ = x_sample.shape
    assert n_s == 1
    NP = page_table.shape[1]
    past = NP * PAGE
    depth = norm_gains.shape[0]
    xp = x_prompt.reshape(B * S, D_MODEL)
    xs = x_sample.reshape(DB, D_MODEL)
    mla_scale = (MLA_NOPE + MLA_ROPE) ** -0.5
    tabs_p = _rope_tables(jnp.tile(jnp.arange(S, dtype=I32), B), mla_scale)
    tabs_s = _rope_tables(jnp.full((DB,), past, I32), mla_scale)
    mla_st = [[] for _ in range(4)]
    dil_st = [[] for _ in range(6)]
    nsa_st = [[] for _ in range(6)]
    ia = ib = ic = 0
    for i in range(depth):
        g = norm_gains[i]
        kind = i % 3
        if kind == 0:
            w = _prep_mla(mla_w_in[ia], mla_q_norm[ia], mla_w_qb[ia], mla_kv_norm[ia], mla_w_kvb[ia], mla_w_o[ia])
            xp, xs, st = _mla_layer(xp, xs, g[0], g[1], w, tabs_p, tabs_s, cache_mla_latent[ia], cache_mla_rope[ia],
                                    page_table, B, S)
            for lst, a in zip(mla_st, st):
                lst.append(a)
            ia += 1
        elif kind == 1:
            xp, xs, st_p, st_s = _dil_layer(xp, xs, g[0], g[1], dil_w_in[ib], dil_w_o[ib],
                                            (cache_dil_kv0[ib], cache_dil_kv1[ib], cache_dil_kv2[ib]), B, S)
            for lst, a in zip(dil_st, (*st_p, *st_s)):
                lst.append(a)
            ib += 1
        else:
            w = _prep_nsa(nsa_w_in[ic], nsa_cmp_w[ic], nsa_cmp_pe[ic], nsa_w_o[ic])
            xp, xs, st_p, st_s = _nsa_layer(xp, xs, g[0], g[1], w, cache_nsa_cmp[ic], cache_nsa_sel[ic],
                                            cache_nsa_win[ic], page_table, B, S)
            for lst, a in zip(nsa_st, (*st_p, *st_s)):
                lst.append(a)
            ic += 1
        w1, w2 = mlp_w1[i].astype(BF16), mlp_w2[i].astype(BF16)
        xp = mlp(xp, g[2], w1, w2, g[3])
        xs = mlp(xs, g[2], w1, w2, g[3])
    stack = lambda lst: jnp.stack(lst)
    return (xp.reshape(B, S, D_MODEL), xs.reshape(DB, 1, D_MODEL),
            *[stack(l) for l in mla_st], *[stack(l) for l in dil_st], *[stack(l) for l in nsa_st])
```
